```python
import jax, jax.numpy as jnp
from jax import lax
import numpy as np

D_MODEL = 1024
BATCH = 8
SEQ = 4096
DEPTH = 1

CHUNK = 64
POOL_WIDTH = D_MODEL // 2
POOL_GROUPS = 4
POOL_GROUP_DIM = POOL_WIDTH // POOL_GROUPS
POOL_WINDOWS = (2, 4, 8, 16)
CONV_WIDTH = D_MODEL // 2
CONV_K = 31
D_FF = 4 * D_MODEL
IN_COLS = POOL_WIDTH + 2 * CONV_WIDTH + 2 * D_MODEL
EPS = 1e-6

kernel_name = "hybrid_pool_conformer_conv_gated_block"


def rmsnorm(x, g):
    xf = x.astype(jnp.float32)
    y = xf * lax.rsqrt(jnp.mean(xf * xf, axis=-1, keepdims=True) + EPS)
    return (y * g.astype(jnp.float32)).astype(x.dtype)


def layernorm(x, g, b):
    xf = x.astype(jnp.float32)
    mu = jnp.mean(xf, axis=-1, keepdims=True)
    var = jnp.mean(jnp.square(xf - mu), axis=-1, keepdims=True)
    y = (xf - mu) * lax.rsqrt(var + EPS)
    return (y * g.astype(jnp.float32) + b.astype(jnp.float32)).astype(x.dtype)


def trailing_mean_minus_self(u, window):
    uf = u.astype(jnp.float32)
    c = jnp.cumsum(uf, axis=1)
    c_prev = jnp.pad(c, ((0, 0), (window, 0), (0, 0)))[:, : u.shape[1], :]
    count = jnp.minimum(jnp.arange(1, u.shape[1] + 1, dtype=jnp.float32), float(window))
    mean = (c - c_prev) / count[None, :, None]
    return (mean - uf).astype(u.dtype)


def pool_mixer(u, pool_w, pool_scale, w_pool_out):
    groups = jnp.split(u, POOL_GROUPS, axis=-1)
    pooled = jnp.stack([trailing_mean_minus_self(g, w) for g, w in zip(groups, POOL_WINDOWS)], axis=2)
    mixed = jnp.einsum('bsgc,gcd->bsgd', pooled, pool_w)
    mixed = mixed.reshape(u.shape) * pool_scale
    return mixed @ w_pool_out


def conformer_conv(glu_in, conv_w, conv_b, ln_g, ln_b, w_conv_out):
    a, gate = jnp.split(glu_in, 2, axis=-1)
    v = a * jax.nn.sigmoid(gate)
    rhs = conv_w.astype(v.dtype)[:, None, :]
    v = lax.conv_general_dilated(v, rhs, window_strides=(1,), padding=[(CONV_K - 1, 0)],
                                 dimension_numbers=('NWC', 'WIO', 'NWC'),
                                 feature_group_count=CONV_WIDTH) + conv_b
    v = layernorm(v, ln_g, ln_b)
    v = jax.nn.silu(v)
    return v @ w_conv_out


def setup_inputs(seed: int = 0) -> dict:
    key = jax.random.key(seed)
    ks = jax.random.split(key, 20)
    L, D = DEPTH, D_MODEL
    nrm = lambda k, shape, fan_in: jax.random.normal(k, shape, jnp.float32) * (fan_in ** -0.5)
    gain = lambda k, shape: 1.0 + 0.05 * jax.random.normal(k, shape, jnp.float32)
    return {
        "x": jax.random.normal(ks[0], (BATCH, SEQ, D), jnp.float32),
        "norm_mix_pre": gain(ks[1], (L, D)),
        "w_in": nrm(ks[2], (L, D, IN_COLS), D),
        "pool_w": nrm(ks[3], (L, POOL_GROUPS, POOL_GROUP_DIM, POOL_GROUP_DIM), POOL_GROUP_DIM),
        "pool_scale": gain(ks[4], (L, POOL_WIDTH)),
        "w_pool_out": nrm(ks[5], (L, POOL_WIDTH, D), POOL_WIDTH),
        "conv_w": nrm(ks[6], (L, CONV_K, CONV_WIDTH), CONV_K),
        "conv_b": 0.02 * jax.random.normal(ks[7], (L, CONV_WIDTH), jnp.float32),
        "conv_ln_g": gain(ks[8], (L, CONV_WIDTH)),
        "conv_ln_b": 0.02 * jax.random.normal(ks[9], (L, CONV_WIDTH), jnp.float32),
        "w_conv_out": nrm(ks[10], (L, CONV_WIDTH, D), CONV_WIDTH),
        "w_o": nrm(ks[11], (L, D, D), D),
        "norm_mix_post": gain(ks[12], (L, D)),
        "norm_mlp_pre": gain(ks[13], (L, D)),
        "w_up": nrm(ks[14], (L, D, D_FF), D),
        "w_down": nrm(ks[15], (L, D_FF, D), D_FF),
        "norm_mlp_post": gain(ks[16], (L, D)),
    }


def reference(x, norm_mix_pre, w_in, pool_w, pool_scale, w_pool_out, conv_w, conv_b,
              conv_ln_g, conv_ln_b, w_conv_out, w_o, norm_mix_post, norm_mlp_pre,
              w_up, w_down, norm_mlp_post):
    for l in range(DEPTH):
        h = rmsnorm(x, norm_mix_pre[l])
        proj = h @ w_in[l]
        u_pool, u_glu, g_a, g_b = jnp.split(
            proj, [POOL_WIDTH, POOL_WIDTH + 2 * CONV_WIDTH, POOL_WIDTH + 2 * CONV_WIDTH + D_MODEL], axis=-1)
        branch_a = pool_mixer(u_pool, pool_w[l], pool_scale[l], w_pool_out[l])
        branch_b = conformer_conv(u_glu, conv_w[l], conv_b[l], conv_ln_g[l], conv_ln_b[l], w_conv_out[l])
        merged = jax.nn.sigmoid(g_a) * branch_a + jax.nn.sigmoid(g_b) * branch_b
        x = x + rmsnorm(merged @ w_o[l], norm_mix_post[l])
        h = rmsnorm(x, norm_mlp_pre[l])
        f = jnp.square(jax.nn.relu(h @ w_up[l])) @ w_down[l]
        x = x + rmsnorm(f, norm_mlp_post[l])
    return x
```

```python
import functools

import jax
import jax.numpy as jnp
from jax import lax
from jax.experimental import pallas as pl
from jax.experimental.pallas import tpu as pltpu

D_MODEL = 1024
POOL_WIDTH = 512
POOL_GROUP_DIM = 128
POOL_WINDOWS = (2, 4, 8, 16)
CONV_WIDTH = 512
CONV_K = 31
D_FF = 4 * D_MODEL
EPS = 1e-6

SUBLANES = 8
LANES = 128
POOL_HALO = 16
CONV_HALO = 32
GLU_OFF = POOL_WIDTH
GATE_A_OFF = POOL_WIDTH + 2 * CONV_WIDTH
GATE_B_OFF = GATE_A_OFF + D_MODEL

MIXER_TILE = 512
MLP_TILE = 512
FF_CHUNK = 1024
CONV_ROWS = 64
VMEM_LIMIT_BYTES = 56 * 1024 * 1024


def _dot(a, b):
    return jnp.dot(a, b, preferred_element_type=jnp.float32)


def _rmsnorm(x, g):
    ms = jnp.mean(x * x, axis=-1, keepdims=True)
    return x * lax.rsqrt(ms + EPS) * g


def _mixer_kernel(x_ref, g_pre_ref, w_in_ref, pool_w_ref, pool_scale_ref, w_pool_out_ref,
                  conv_w_ref, conv_b_ref, ln_g_ref, ln_b_ref, w_conv_out_ref, w_o_ref,
                  g_post_ref, o_ref, pool_ext, conv_ext, conv_out, *, tile):
    t = pl.program_id(1)

    @pl.when(t == 0)
    def _():
        pool_ext[0:POOL_HALO, :] = jnp.zeros((POOL_HALO, POOL_WIDTH), jnp.float32)
        conv_ext[0:CONV_HALO, :] = jnp.zeros((CONV_HALO, CONV_WIDTH), jnp.float32)

    x = x_ref[...]
    h = _rmsnorm(x, g_pre_ref[...]).astype(jnp.bfloat16)

    u_pool = _dot(h, w_in_ref[:, 0:POOL_WIDTH])
    pool_ext[POOL_HALO:POOL_HALO + tile, :] = u_pool
    frame = t * tile + lax.broadcasted_iota(jnp.int32, (tile, 1), 0)
    mixed = []
    for g, window in enumerate(POOL_WINDOWS):
        cols = slice(g * POOL_GROUP_DIM, (g + 1) * POOL_GROUP_DIM)
        u_g = pool_ext[POOL_HALO:POOL_HALO + tile, cols]
        s = u_g
        for j in range(1, window):
            s = s + pool_ext[POOL_HALO - j:POOL_HALO - j + tile, cols]
        inv_count = 1.0 / jnp.minimum(frame + 1, window).astype(jnp.float32)
        pooled = s * inv_count - u_g
        mixed.append(_dot(pooled.astype(jnp.bfloat16), pool_w_ref[g]))
    mixed = jnp.concatenate(mixed, axis=-1) * pool_scale_ref[...]
    branch_a = _dot(mixed.astype(jnp.bfloat16), w_pool_out_ref[...])

    glu = _dot(h, w_in_ref[:, GLU_OFF:GLU_OFF + 2 * CONV_WIDTH])
    v = glu[:, 0:CONV_WIDTH] * jax.nn.sigmoid(glu[:, CONV_WIDTH:2 * CONV_WIDTH])
    conv_ext[CONV_HALO:CONV_HALO + tile, :] = v
    first_tap = CONV_HALO - (CONV_K - 1)
    for r0 in range(0, tile, CONV_ROWS):
        for c0 in range(0, CONV_WIDTH, LANES):
            cols = slice(c0, c0 + LANES)
            acc = jnp.broadcast_to(conv_b_ref[:, cols], (CONV_ROWS, LANES))
            for k in range(CONV_K):
                lo = first_tap + k + r0
                acc = acc + conv_w_ref[k:k + 1, cols] * conv_ext[lo:lo + CONV_ROWS, cols]
            conv_out[r0:r0 + CONV_ROWS, cols] = acc
    c = conv_out[...]
    mu = jnp.mean(c, axis=-1, keepdims=True)
    cc = c - mu
    var = jnp.mean(cc * cc, axis=-1, keepdims=True)
    y = cc * lax.rsqrt(var + EPS) * ln_g_ref[...] + ln_b_ref[...]
    y = y * jax.nn.sigmoid(y)
    branch_b = _dot(y.astype(jnp.bfloat16), w_conv_out_ref[...])

    gate_a = _dot(h, w_in_ref[:, GATE_A_OFF:GATE_A_OFF + D_MODEL])
    gate_b = _dot(h, w_in_ref[:, GATE_B_OFF:GATE_B_OFF + D_MODEL])
    merged = jax.nn.sigmoid(gate_a) * branch_a + jax.nn.sigmoid(gate_b) * branch_b
    mix = _dot(merged.astype(jnp.bfloat16), w_o_ref[...])
    o_ref[...] = x + _rmsnorm(mix, g_post_ref[...])

    pool_ext[0:POOL_HALO, :] = pool_ext[tile:tile + POOL_HALO, :]
    conv_ext[0:CONV_HALO, :] = conv_ext[tile:tile + CONV_HALO, :]


def _mlp_kernel(x_ref, g_pre_ref, w_up_ref, w_down_ref, g_post_ref, o_ref):
    x = x_ref[...]
    h = _rmsnorm(x, g_pre_ref[...]).astype(jnp.bfloat16)
    acc = None
    for c0 in range(0, D_FF, FF_CHUNK):
        a = jnp.maximum(_dot(h, w_up_ref[:, c0:c0 + FF_CHUNK]), 0.0)
        p = _dot((a * a).astype(jnp.bfloat16), w_down_ref[c0:c0 + FF_CHUNK, :])
        acc = p if acc is None else acc + p
    o_ref[...] = x + _rmsnorm(acc, g_post_ref[...])


def _resident(shape):
    return pl.BlockSpec(shape, lambda *_: (0,) * len(shape), pipeline_mode=pl.Buffered(1))


def _mixer(x, g_pre, w_in, pool_w, pool_scale, w_pool_out, conv_w, conv_b, ln_g, ln_b,
           w_conv_out, w_o, g_post):
    batch, seq, d = x.shape
    tile = MIXER_TILE
    assert seq % tile == 0 and d == D_MODEL
    x_spec = pl.BlockSpec((None, tile, d), lambda b, t: (b, t, 0))
    operands = (g_pre, w_in, pool_w, pool_scale, w_pool_out, conv_w, conv_b, ln_g, ln_b,
                w_conv_out, w_o, g_post)
    return pl.pallas_call(
        functools.partial(_mixer_kernel, tile=tile),
        grid=(batch, seq // tile),
        in_specs=[x_spec] + [_resident(a.shape) for a in operands],
        out_specs=x_spec,
        out_shape=jax.ShapeDtypeStruct(x.shape, x.dtype),
        scratch_shapes=[
            pltpu.VMEM((POOL_HALO + tile, POOL_WIDTH), jnp.float32),
            pltpu.VMEM((CONV_HALO + tile, CONV_WIDTH), jnp.float32),
            pltpu.VMEM((tile, CONV_WIDTH), jnp.float32),
        ],
        compiler_params=pltpu.CompilerParams(
            dimension_semantics=("arbitrary", "arbitrary"),
            vmem_limit_bytes=VMEM_LIMIT_BYTES),
        name="mixer",
    )(x, *operands)


def _mlp(x, g_pre, w_up, w_down, g_post):
    tokens, d = x.shape
    tile = MLP_TILE
    assert tokens % tile == 0 and d == D_MODEL
    x_spec = pl.BlockSpec((tile, d), lambda i: (i, 0))
    operands = (g_pre, w_up, w_down, g_post)
    return pl.pallas_call(
        _mlp_kernel,
        grid=(tokens // tile,),
        in_specs=[x_spec] + [_resident(a.shape) for a in operands],
        out_specs=x_spec,
        out_shape=jax.ShapeDtypeStruct(x.shape, x.dtype),
        compiler_params=pltpu.CompilerParams(
            dimension_semantics=("arbitrary",),
            vmem_limit_bytes=VMEM_LIMIT_BYTES),
        name="mlp",
    )(x, *operands)


def kernel(x, norm_mix_pre, w_in, pool_w, pool_scale, w_pool_out, conv_w, conv_b, conv_ln_g,
           conv_ln_b, w_conv_out, w_o, norm_mix_post, norm_mlp_pre, w_up, w_down, norm_mlp_post):
    batch, seq, d = x.shape
    bf16 = lambda w: w.astype(jnp.bfloat16)
    row = lambda p: p.reshape(1, -1)
    for l in range(w_in.shape[0]):
        x = _mixer(x, row(norm_mix_pre[l]), bf16(w_in[l]), bf16(pool_w[l]), row(pool_scale[l]),
                   bf16(w_pool_out[l]), conv_w[l], row(conv_b[l]), row(conv_ln_g[l]),
                   row(conv_ln_b[l]), bf16(w_conv_out[l]), bf16(w_o[l]), row(norm_mix_post[l]))
        x = _mlp(x.reshape(batch * seq, d), row(norm_mlp_pre[l]), bf16(w_up[l]), bf16(w_down[l]),
                 row(norm_mlp_post[l])).reshape(batch, seq, d)
    return x
```

```python
import functools

import jax
import jax.numpy as jnp
from jax import lax
from jax.experimental import pallas as pl
from jax.experimental.pallas import tpu as pltpu

D_MODEL = 1024
POOL_WIDTH = 512
POOL_GROUP_DIM = 128
POOL_WINDOWS = (2, 4, 8, 16)
CONV_WIDTH = 512
CONV_K = 31
D_FF = 4 * D_MODEL
EPS = 1e-6

SUBLANES = 8
LANES = 128
POOL_HALO = 16
CONV_HALO = 32
GLU_OFF = POOL_WIDTH
GATE_A_OFF = POOL_WIDTH + 2 * CONV_WIDTH

MIXER_TILE = 512
MLP_TILE = 512
FF_CHUNK = 1024
PROJ_COLS = 512
CONV_ROWS = 64
VMEM_LIMIT_BYTES = 56 * 1024 * 1024


def _dot(a, b):
    return jnp.dot(a, b, preferred_element_type=jnp.float32)


def _rmsnorm(x, g):
    ms = jnp.mean(x * x, axis=-1, keepdims=True)
    return x * lax.rsqrt(ms + EPS) * g


def _mixer_kernel(x_ref, g_pre_ref, w_in_ref, pool_w_ref, pool_scale_ref, w_pool_out_ref,
                  conv_w_ref, conv_b_ref, ln_g_ref, ln_b_ref, w_conv_out_ref, w_o_ref,
                  g_post_ref, o_ref,
                  h_ref, pool_ext, mixed_ref, glu_ref, conv_x, y_ref, gate_ref, a_ref, merged_ref,
                  *, tile):
    t = pl.program_id(1)
    conv_rows = CONV_HALO + tile

    @pl.when(t == 0)
    def _():
        pool_ext[0:POOL_HALO, :] = jnp.zeros((POOL_HALO, POOL_WIDTH), jnp.float32)
        conv_x[0, 0:CONV_HALO, :] = jnp.zeros((CONV_HALO, CONV_WIDTH), jnp.float32)

    def project(c0):
        return _dot(h_ref[...], w_in_ref[:, c0:c0 + PROJ_COLS])

    h_ref[...] = _rmsnorm(x_ref[...], g_pre_ref[...]).astype(jnp.bfloat16)

    glu_ref[:, 0:CONV_WIDTH] = project(GLU_OFF)
    glu_ref[:, CONV_WIDTH:2 * CONV_WIDTH] = project(GLU_OFF + CONV_WIDTH)
    for r0 in range(0, tile, CONV_ROWS):
        rows = slice(r0, r0 + CONV_ROWS)
        conv_x[0, CONV_HALO + r0:CONV_HALO + r0 + CONV_ROWS, :] = (
            glu_ref[rows, 0:CONV_WIDTH]
            * jax.nn.sigmoid(glu_ref[rows, CONV_WIDTH:2 * CONV_WIDTH]))
        if r0 == CONV_ROWS:
            pool_ext[POOL_HALO:POOL_HALO + tile, :] = project(0)

    for s in range(1, SUBLANES):
        for c0 in range(0, CONV_WIDTH, LANES):
            cols = slice(c0, c0 + LANES)
            conv_x[s, :, cols] = pltpu.roll(conv_x[0, :, cols], conv_rows - s, axis=0)
        if s % 2 == 1:
            c0 = (s // 2) * PROJ_COLS
            gate_ref[:, c0:c0 + PROJ_COLS] = project(GATE_A_OFF + c0)

    frame = t * tile + lax.broadcasted_iota(jnp.int32, (tile, 1), 0)
    for g, window in enumerate(POOL_WINDOWS):
        cols = slice(g * POOL_GROUP_DIM, (g + 1) * POOL_GROUP_DIM)
        ext = pool_ext[:, cols]
        s = ext
        step = 1
        while step < window:
            s = s + pltpu.roll(s, step, axis=0)
            step *= 2
        inv_count = 1.0 / jnp.minimum(frame + 1, window).astype(jnp.float32)
        pooled = s[POOL_HALO:, :] * inv_count - ext[POOL_HALO:, :]
        mixed_g = _dot(pooled.astype(jnp.bfloat16), pool_w_ref[g]) * pool_scale_ref[:, cols]
        mixed_ref[:, cols] = mixed_g.astype(jnp.bfloat16)

    first_tap = CONV_HALO - (CONV_K - 1)
    for r0 in range(0, tile, CONV_ROWS):
        accs = []
        for c0 in range(0, CONV_WIDTH, LANES):
            cols = slice(c0, c0 + LANES)
            acc = jnp.broadcast_to(conv_b_ref[:, cols], (CONV_ROWS, LANES))
            for k in range(CONV_K):
                q, s = divmod(first_tap + k, SUBLANES)
                lo = r0 + q * SUBLANES
                acc = acc + conv_w_ref[k:k + 1, cols] * conv_x[s, lo:lo + CONV_ROWS, cols]
            accs.append(acc)
        c = jnp.concatenate(accs, axis=-1)
        mu = jnp.mean(c, axis=-1, keepdims=True)
        cc = c - mu
        var = jnp.mean(cc * cc, axis=-1, keepdims=True)
        y = cc * lax.rsqrt(var + EPS) * ln_g_ref[...] + ln_b_ref[...]
        y_ref[r0:r0 + CONV_ROWS, :] = (y * jax.nn.sigmoid(y)).astype(jnp.bfloat16)
        if r0 // CONV_ROWS in (1, 3):
            c0 = (r0 // CONV_ROWS // 2) * PROJ_COLS
            a_ref[:, c0:c0 + PROJ_COLS] = _dot(mixed_ref[...], w_pool_out_ref[:, c0:c0 + PROJ_COLS])

    for c0 in range(0, D_MODEL, PROJ_COLS):
        cols = slice(c0, c0 + PROJ_COLS)
        b = _dot(y_ref[...], w_conv_out_ref[:, cols])
        merged = (jax.nn.sigmoid(gate_ref[:, cols]) * a_ref[:, cols]
                  + jax.nn.sigmoid(gate_ref[:, D_MODEL + c0:D_MODEL + c0 + PROJ_COLS]) * b)
        merged_ref[:, cols] = merged.astype(jnp.bfloat16)
    mix = _dot(merged_ref[...], w_o_ref[...])
    o_ref[...] = x_ref[...] + _rmsnorm(mix, g_post_ref[...])

    pool_ext[0:POOL_HALO, :] = pool_ext[tile:tile + POOL_HALO, :]
    conv_x[0, 0:CONV_HALO, :] = conv_x[0, tile:tile + CONV_HALO, :]


def _mlp_kernel(x_ref, g_pre_ref, w_up_ref, w_down_ref, g_post_ref, o_ref):
    x = x_ref[...]
    h = _rmsnorm(x, g_pre_ref[...]).astype(jnp.bfloat16)
    acc = None
    for c0 in range(0, D_FF, FF_CHUNK):
        a = jnp.maximum(_dot(h, w_up_ref[:, c0:c0 + FF_CHUNK]), 0.0)
        p = _dot((a * a).astype(jnp.bfloat16), w_down_ref[c0:c0 + FF_CHUNK, :])
        acc = p if acc is None else acc + p
    o_ref[...] = x + _rmsnorm(acc, g_post_ref[...])


def _resident(shape):
    return pl.BlockSpec(shape, lambda *_: (0,) * len(shape), pipeline_mode=pl.Buffered(1))


def _mixer(x, g_pre, w_in, pool_w, pool_scale, w_pool_out, conv_w, conv_b, ln_g, ln_b,
           w_conv_out, w_o, g_post):
    batch, seq, d = x.shape
    tile = MIXER_TILE
    assert seq % tile == 0 and d == D_MODEL
    x_spec = pl.BlockSpec((None, tile, d), lambda b, t: (b, t, 0))
    operands = (g_pre, w_in, pool_w, pool_scale, w_pool_out, conv_w, conv_b, ln_g, ln_b,
                w_conv_out, w_o, g_post)
    return pl.pallas_call(
        functools.partial(_mixer_kernel, tile=tile),
        grid=(batch, seq // tile),
        in_specs=[x_spec] + [_resident(a.shape) for a in operands],
        out_specs=x_spec,
        out_shape=jax.ShapeDtypeStruct(x.shape, x.dtype),
        scratch_shapes=[
            pltpu.VMEM((tile, d), jnp.bfloat16),
            pltpu.VMEM((POOL_HALO + tile, POOL_WIDTH), jnp.float32),
            pltpu.VMEM((tile, POOL_WIDTH), jnp.bfloat16),
            pltpu.VMEM((tile, 2 * CONV_WIDTH), jnp.float32),
            pltpu.VMEM((SUBLANES, CONV_HALO + tile, CONV_WIDTH), jnp.float32),
            pltpu.VMEM((tile, CONV_WIDTH), jnp.bfloat16),
            pltpu.VMEM((tile, 2 * d), jnp.float32),
            pltpu.VMEM((tile, d), jnp.float32),
            pltpu.VMEM((tile, d), jnp.bfloat16),
        ],
        compiler_params=pltpu.CompilerParams(
            dimension_semantics=("arbitrary", "arbitrary"),
            vmem_limit_bytes=VMEM_LIMIT_BYTES),
        name="mixer",
    )(x, *operands)


def _mlp(x, g_pre, w_up, w_down, g_post):
    tokens, d = x.shape
    tile = MLP_TILE
    assert tokens % tile == 0 and d == D_MODEL
    x_spec = pl.BlockSpec((tile, d), lambda i: (i, 0))
    operands = (g_pre, w_up, w_down, g_post)
    return pl.pallas_call(
        _mlp_kernel,
        grid=(tokens // tile,),
        in_specs=[x_spec] + [_resident(a.shape) for a in operands],
        out_specs=x_spec,
        out_shape=jax.ShapeDtypeStruct(x.shape, x.dtype),
        compiler_params=pltpu.CompilerParams(
            dimension_semantics=("arbitrary",),
            vmem_limit_bytes=VMEM_LIMIT_BYTES),
        name="mlp",
    )(x, *operands)


def kernel(x, norm_mix_pre, w_in, pool_w, pool_scale, w_pool_out, conv_w, conv_b, conv_ln_g,
           conv_ln_b, w_conv_out, w_o, norm_mix_post, norm_mlp_pre, w_up, w_down, norm_mlp_post):
    batch, seq, d = x.shape
    bf16 = lambda w: w.astype(jnp.bfloat16)
    row = lambda p: p.reshape(1, -1)
    for l in range(w_in.shape[0]):
        x = _mixer(x, row(norm_mix_pre[l]), bf16(w_in[l]), bf16(pool_w[l]), row(pool_scale[l]),
                   bf16(w_pool_out[l]), conv_w[l], row(conv_b[l]), row(conv_ln_g[l]),
                   row(conv_ln_b[l]), bf16(w_conv_out[l]), bf16(w_o[l]), row(norm_mix_post[l]))
        x = _mlp(x.reshape(batch * seq, d), row(norm_mlp_pre[l]), bf16(w_up[l]), bf16(w_down[l]),
                 row(norm_mlp_post[l])).reshape(batch, seq, d)
    return x
```

```python
import functools

import jax
import jax.numpy as jnp
from jax import lax
from jax.experimental import pallas as pl
from jax.experimental.pallas import tpu as pltpu

D_MODEL = 1024
POOL_WIDTH = 512
POOL_GROUP_DIM = 128
POOL_WINDOWS = (2, 4, 8, 16)
CONV_WIDTH = 512
CONV_K = 31
D_FF = 4 * D_MODEL
EPS = 1e-6

SUBLANES = 8
LANES = 128
POOL_HALO = 16
CONV_HALO = 32
GLU_OFF = POOL_WIDTH
GATE_A_OFF = POOL_WIDTH + 2 * CONV_WIDTH

MIXER_TILE = 512
MLP_TILE = 512
HALF_ROWS = 256
FF_CHUNK = 1024
PROJ_COLS = 512
CONV_ROWS = 64
VMEM_LIMIT_BYTES = 56 * 1024 * 1024


def _dot(a, b):
    return jnp.dot(a, b, preferred_element_type=jnp.float32)


def _rmsnorm(x, g):
    ms = jnp.mean(x * x, axis=-1, keepdims=True)
    return x * lax.rsqrt(ms + EPS) * g


def _shift_rows(x, k):
    n = x.shape[0] // SUBLANES
    if abs(k) == SUBLANES:
        return jnp.concatenate([x[k:], x[:k]], axis=0)
    x3 = x.reshape(n, SUBLANES, LANES)
    sub = lax.broadcasted_iota(jnp.int32, (1, SUBLANES, LANES), 1)
    r = pltpu.roll(x3, (-k) % SUBLANES, axis=1)
    if k > 0:
        other = jnp.concatenate([r[1:], r[:1]], axis=0)
        y = jnp.where(sub < SUBLANES - k, r, other)
    else:
        other = jnp.concatenate([r[-1:], r[:-1]], axis=0)
        y = jnp.where(sub >= -k, r, other)
    return y.reshape(x.shape)


def _mixer_kernel(x_ref, g_pre_ref, w_in_ref, pool_w_ref, pool_scale_ref, w_pool_out_ref,
                  conv_w_ref, conv_b_ref, ln_g_ref, ln_b_ref, w_conv_out_ref, w_o_ref,
                  g_post_ref, o_ref,
                  h_ref, pool_ext, mixed_ref, glu_ref, conv_x, y_ref, gate_ref, a_ref, merged_ref,
                  *, tile):
    t = pl.program_id(1)

    @pl.when(t == 0)
    def _():
        pool_ext[0:POOL_HALO, :] = jnp.zeros((POOL_HALO, POOL_WIDTH), jnp.float32)
        conv_x[0, 0:CONV_HALO, :] = jnp.zeros((CONV_HALO, CONV_WIDTH), jnp.float32)

    def project(c0):
        return _dot(h_ref[...], w_in_ref[:, c0:c0 + PROJ_COLS])

    for r0 in range(0, tile, HALF_ROWS):
        rows = slice(r0, r0 + HALF_ROWS)
        h_half = _rmsnorm(x_ref[rows, :], g_pre_ref[...]).astype(jnp.bfloat16)
        h_ref[rows, :] = h_half
        glu_ref[rows, :] = _dot(h_half, w_in_ref[:, GLU_OFF:GLU_OFF + 2 * CONV_WIDTH])

    for r0 in range(0, tile, CONV_ROWS):
        rows = slice(r0, r0 + CONV_ROWS)
        conv_x[0, CONV_HALO + r0:CONV_HALO + r0 + CONV_ROWS, :] = (
            glu_ref[rows, 0:CONV_WIDTH]
            * jax.nn.sigmoid(glu_ref[rows, CONV_WIDTH:2 * CONV_WIDTH]))
        if r0 == CONV_ROWS:
            pool_ext[POOL_HALO:POOL_HALO + tile, :] = project(0)

    for c0 in range(0, CONV_WIDTH, LANES):
        cols = slice(c0, c0 + LANES)
        x0 = conv_x[0, :, cols]
        for s in range(1, SUBLANES):
            conv_x[s, :, cols] = _shift_rows(x0, s)
    for c0 in range(0, 2 * D_MODEL, PROJ_COLS):
        gate_ref[:, c0:c0 + PROJ_COLS] = project(GATE_A_OFF + c0)

    frame = t * tile + lax.broadcasted_iota(jnp.int32, (tile, 1), 0)
    for g, window in enumerate(POOL_WINDOWS):
        cols = slice(g * POOL_GROUP_DIM, (g + 1) * POOL_GROUP_DIM)
        ext = pool_ext[:, cols]
        s = ext
        step = 1
        while step < window:
            s = s + _shift_rows(s, -step)
            step *= 2
        inv_count = 1.0 / jnp.minimum(frame + 1, window).astype(jnp.float32)
        pooled = s[POOL_HALO:, :] * inv_count - ext[POOL_HALO:, :]
        mixed_g = _dot(pooled.astype(jnp.bfloat16), pool_w_ref[g]) * pool_scale_ref[:, cols]
        mixed_ref[:, cols] = mixed_g.astype(jnp.bfloat16)

    first_tap = CONV_HALO - (CONV_K - 1)
    for r0 in range(0, tile, CONV_ROWS):
        accs = []
        for c0 in range(0, CONV_WIDTH, LANES):
            cols = slice(c0, c0 + LANES)
            acc = jnp.broadcast_to(conv_b_ref[:, cols], (CONV_ROWS, LANES))
            for k in range(CONV_K):
                q, s = divmod(first_tap + k, SUBLANES)
                lo = r0 + q * SUBLANES
                acc = acc + conv_w_ref[k:k + 1, cols] * conv_x[s, lo:lo + CONV_ROWS, cols]
            accs.append(acc)
        c = jnp.concatenate(accs, axis=-1)
        mu = jnp.mean(c, axis=-1, keepdims=True)
        cc = c - mu
        var = jnp.mean(cc * cc, axis=-1, keepdims=True)
        y = cc * lax.rsqrt(var + EPS) * ln_g_ref[...] + ln_b_ref[...]
        y_ref[r0:r0 + CONV_ROWS, :] = (y * jax.nn.sigmoid(y)).astype(jnp.bfloat16)
        if r0 // CONV_ROWS in (1, 3):
            c0 = (r0 // CONV_ROWS // 2) * PROJ_COLS
            a_ref[:, c0:c0 + PROJ_COLS] = _dot(mixed_ref[...], w_pool_out_ref[:, c0:c0 + PROJ_COLS])

    for r0 in range(0, tile, HALF_ROWS):
        rows = slice(r0, r0 + HALF_ROWS)
        for c0 in range(0, D_MODEL, PROJ_COLS):
            cols = slice(c0, c0 + PROJ_COLS)
            b = _dot(y_ref[rows, :], w_conv_out_ref[:, cols])
            merged = (jax.nn.sigmoid(gate_ref[rows, cols]) * a_ref[rows, cols]
                      + jax.nn.sigmoid(gate_ref[rows, D_MODEL + c0:D_MODEL + c0 + PROJ_COLS]) * b)
            merged_ref[rows, cols] = merged.astype(jnp.bfloat16)
        mix = _dot(merged_ref[rows, :], w_o_ref[...])
        o_ref[rows, :] = x_ref[rows, :] + _rmsnorm(mix, g_post_ref[...])

    pool_ext[0:POOL_HALO, :] = pool_ext[tile:tile + POOL_HALO, :]
    conv_x[0, 0:CONV_HALO, :] = conv_x[0, tile:tile + CONV_HALO, :]


def _mlp_kernel(x_ref, g_pre_ref, w_up_ref, w_down_ref, g_post_ref, o_ref):
    for r0 in range(0, x_ref.shape[0], HALF_ROWS):
        rows = slice(r0, r0 + HALF_ROWS)
        x = x_ref[rows, :]
        h = _rmsnorm(x, g_pre_ref[...]).astype(jnp.bfloat16)
        acc = None
        for c0 in range(0, D_FF, FF_CHUNK):
            a = jnp.maximum(_dot(h, w_up_ref[:, c0:c0 + FF_CHUNK]), 0.0)
            p = _dot((a * a).astype(jnp.bfloat16), w_down_ref[c0:c0 + FF_CHUNK, :])
            acc = p if acc is None else acc + p
        o_ref[rows, :] = x + _rmsnorm(acc, g_post_ref[...])


def _resident(shape):
    return pl.BlockSpec(shape, lambda *_: (0,) * len(shape), pipeline_mode=pl.Buffered(1))


def _mixer(x, g_pre, w_in, pool_w, pool_scale, w_pool_out, conv_w, conv_b, ln_g, ln_b,
           w_conv_out, w_o, g_post):
    batch, seq, d = x.shape
    tile = MIXER_TILE
    assert seq % tile == 0 and d == D_MODEL
    x_spec = pl.BlockSpec((None, tile, d), lambda b, t: (b, t, 0))
    operands = (g_pre, w_in, pool_w, pool_scale, w_pool_out, conv_w, conv_b, ln_g, ln_b,
                w_conv_out, w_o, g_post)
    return pl.pallas_call(
        functools.partial(_mixer_kernel, tile=tile),
        grid=(batch, seq // tile),
        in_specs=[x_spec] + [_resident(a.shape) for a in operands],
        out_specs=x_spec,
        out_shape=jax.ShapeDtypeStruct(x.shape, x.dtype),
        scratch_shapes=[
            pltpu.VMEM((tile, d), jnp.bfloat16),
            pltpu.VMEM((POOL_HALO + tile, POOL_WIDTH), jnp.float32),
            pltpu.VMEM((tile, POOL_WIDTH), jnp.bfloat16),
            pltpu.VMEM((tile, 2 * CONV_WIDTH), jnp.float32),
            pltpu.VMEM((SUBLANES, CONV_HALO + tile, CONV_WIDTH), jnp.float32),
            pltpu.VMEM((tile, CONV_WIDTH), jnp.bfloat16),
            pltpu.VMEM((tile, 2 * d), jnp.float32),
            pltpu.VMEM((tile, d), jnp.float32),
            pltpu.VMEM((tile, d), jnp.bfloat16),
        ],
        compiler_params=pltpu.CompilerParams(
            dimension_semantics=("arbitrary", "arbitrary"),
            vmem_limit_bytes=VMEM_LIMIT_BYTES),
        name="mixer",
    )(x, *operands)


def _mlp(x, g_pre, w_up, w_down, g_post):
    tokens, d = x.shape
    tile = MLP_TILE
    assert tokens % tile == 0 and d == D_MODEL
    x_spec = pl.BlockSpec((tile, d), lambda i: (i, 0))
    operands = (g_pre, w_up, w_down, g_post)
    return pl.pallas_call(
        _mlp_kernel,
        grid=(tokens // tile,),
        in_specs=[x_spec] + [_resident(a.shape) for a in operands],
        out_specs=x_spec,
        out_shape=jax.ShapeDtypeStruct(x.shape, x.dtype),
        compiler_params=pltpu.CompilerParams(
            dimension_semantics=("arbitrary",),
            vmem_limit_bytes=VMEM_LIMIT_BYTES),
        name="mlp",
    )(x, *operands)


def kernel(x, norm_mix_pre, w_in, pool_w, pool_scale, w_pool_out, conv_w, conv_b, conv_ln_g,
           conv_ln_b, w_conv_out, w_o, norm_mix_post, norm_mlp_pre, w_up, w_down, norm_mlp_post):
    batch, seq, d = x.shape
    bf16 = lambda w: w.astype(jnp.bfloat16)
    row = lambda p: p.reshape(1, -1)
    for l in range(w_in.shape[0]):
        x = _mixer(x, row(norm_mix_pre[l]), bf16(w_in[l]), bf16(pool_w[l]), row(pool_scale[l]),
                   bf16(w_pool_out[l]), conv_w[l], row(conv_b[l]), row(conv_ln_g[l]),
                   row(conv_ln_b[l]), bf16(w_conv_out[l]), bf16(w_o[l]), row(norm_mix_post[l]))
        x = _mlp(x.reshape(batch * seq, d), row(norm_mlp_pre[l]), bf16(w_up[l]), bf16(w_down[l]),
                 row(norm_mlp_post[l])).reshape(batch, seq, d)
    return x
```

```python
import functools

import jax
import jax.numpy as jnp
from jax import lax
from jax.experimental import pallas as pl
from jax.experimental.pallas import tpu as pltpu

D_MODEL = 1024
POOL_WIDTH = 512
POOL_GROUP_DIM = 128
POOL_WINDOWS = (2, 4, 8, 16)
CONV_WIDTH = 512
CONV_K = 31
D_FF = 4 * D_MODEL
EPS = 1e-6

SUBLANES = 8
LANES = 128
POOL_HALO = 16
CONV_HALO = 32
GLU_OFF = POOL_WIDTH
GATE_A_OFF = POOL_WIDTH + 2 * CONV_WIDTH

MIXER_TILE = 512
MLP_TILE = 1024
FF_CHUNK = 1024
PROJ_COLS = 512
CONV_ROWS = 64
VMEM_LIMIT_BYTES = 56 * 1024 * 1024


def _dot(a, b):
    return jnp.dot(a, b, preferred_element_type=jnp.float32)


def _rmsnorm(x, g):
    ms = jnp.mean(x * x, axis=-1, keepdims=True)
    return x * lax.rsqrt(ms + EPS) * g


def _mixer_kernel(x_ref, g_pre_ref, w_in_ref, pool_w_ref, pool_scale_ref, w_pool_out_ref,
                  conv_w_ref, conv_b_ref, ln_g_ref, ln_b_ref, w_conv_out_ref, w_o_ref,
                  g_post_ref, o_ref,
                  h_ref, pool_ext, mixed_ref, glu_ref, conv_x, y_ref, gate_ref, a_ref, merged_ref,
                  *, tile):
    t = pl.program_id(1)
    conv_rows = CONV_HALO + tile

    @pl.when(t == 0)
    def _():
        pool_ext[0:POOL_HALO, :] = jnp.zeros((POOL_HALO, POOL_WIDTH), jnp.float32)
        conv_x[0, 0:CONV_HALO, :] = jnp.zeros((CONV_HALO, CONV_WIDTH), jnp.float32)

    def project(c0):
        return _dot(h_ref[...], w_in_ref[:, c0:c0 + PROJ_COLS])

    h_ref[...] = _rmsnorm(x_ref[...], g_pre_ref[...]).astype(jnp.bfloat16)

    glu_ref[:, 0:CONV_WIDTH] = project(GLU_OFF)
    glu_ref[:, CONV_WIDTH:2 * CONV_WIDTH] = project(GLU_OFF + CONV_WIDTH)
    for r0 in range(0, tile, CONV_ROWS):
        rows = slice(r0, r0 + CONV_ROWS)
        conv_x[0, CONV_HALO + r0:CONV_HALO + r0 + CONV_ROWS, :] = (
            glu_ref[rows, 0:CONV_WIDTH]
            * jax.nn.sigmoid(glu_ref[rows, CONV_WIDTH:2 * CONV_WIDTH]))
        if r0 == CONV_ROWS:
            pool_ext[POOL_HALO:POOL_HALO + tile, :] = project(0)

    for s in range(1, SUBLANES):
        for c0 in range(0, CONV_WIDTH, LANES):
            cols = slice(c0, c0 + LANES)
            conv_x[s, :, cols] = pltpu.roll(conv_x[0, :, cols], conv_rows - s, axis=0)
        if s % 2 == 1:
            c0 = (s // 2) * PROJ_COLS
            gate_ref[:, c0:c0 + PROJ_COLS] = project(GATE_A_OFF + c0)

    frame = t * tile + lax.broadcasted_iota(jnp.int32, (tile, 1), 0)
    for g, window in enumerate(POOL_WINDOWS):
        cols = slice(g * POOL_GROUP_DIM, (g + 1) * POOL_GROUP_DIM)
        ext = pool_ext[:, cols]
        s = ext
        step = 1
        while step < window:
            s = s + pltpu.roll(s, step, axis=0)
            step *= 2
        inv_count = 1.0 / jnp.minimum(frame + 1, window).astype(jnp.float32)
        pooled = s[POOL_HALO:, :] * inv_count - ext[POOL_HALO:, :]
        mixed_g = _dot(pooled.astype(jnp.bfloat16), pool_w_ref[g]) * pool_scale_ref[:, cols]
        mixed_ref[:, cols] = mixed_g.astype(jnp.bfloat16)

    first_tap = CONV_HALO - (CONV_K - 1)
    for r0 in range(0, tile, CONV_ROWS):
        accs = []
        for c0 in range(0, CONV_WIDTH, LANES):
            cols = slice(c0, c0 + LANES)
            acc = jnp.broadcast_to(conv_b_ref[:, cols], (CONV_ROWS, LANES))
            for k in range(CONV_K):
                q, s = divmod(first_tap + k, SUBLANES)
                lo = r0 + q * SUBLANES
                acc = acc + conv_w_ref[k:k + 1, cols] * conv_x[s, lo:lo + CONV_ROWS, cols]
            accs.append(acc)
        c = jnp.concatenate(accs, axis=-1)
        mu = jnp.mean(c, axis=-1, keepdims=True)
        cc = c - mu
        var = jnp.mean(cc * cc, axis=-1, keepdims=True)
        y = cc * lax.rsqrt(var + EPS) * ln_g_ref[...] + ln_b_ref[...]
        y_ref[r0:r0 + CONV_ROWS, :] = (y * jax.nn.sigmoid(y)).astype(jnp.bfloat16)
        if r0 // CONV_ROWS in (1, 3):
            c0 = (r0 // CONV_ROWS // 2) * PROJ_COLS
            a_ref[:, c0:c0 + PROJ_COLS] = _dot(mixed_ref[...], w_pool_out_ref[:, c0:c0 + PROJ_COLS])

    for c0 in range(0, D_MODEL, PROJ_COLS):
        cols = slice(c0, c0 + PROJ_COLS)
        b = _dot(y_ref[...], w_conv_out_ref[:, cols])
        merged = (jax.nn.sigmoid(gate_ref[:, cols]) * a_ref[:, cols]
                  + jax.nn.sigmoid(gate_ref[:, D_MODEL + c0:D_MODEL + c0 + PROJ_COLS]) * b)
        merged_ref[:, cols] = merged.astype(jnp.bfloat16)
    mix = _dot(merged_ref[...], w_o_ref[...])
    o_ref[...] = x_ref[...] + _rmsnorm(mix, g_post_ref[...])

    pool_ext[0:POOL_HALO, :] = pool_ext[tile:tile + POOL_HALO, :]
    conv_x[0, 0:CONV_HALO, :] = conv_x[0, tile:tile + CONV_HALO, :]


def _mlp_kernel(x_ref, g_pre_ref, w_up_ref, w_down_ref, g_post_ref, o_ref):
    x = x_ref[...]
    h = _rmsnorm(x, g_pre_ref[...]).astype(jnp.bfloat16)
    acc = None
    for c0 in range(0, D_FF, FF_CHUNK):
        a = jnp.maximum(_dot(h, w_up_ref[:, c0:c0 + FF_CHUNK]), 0.0)
        p = _dot((a * a).astype(jnp.bfloat16), w_down_ref[c0:c0 + FF_CHUNK, :])
        acc = p if acc is None else acc + p
    o_ref[...] = x + _rmsnorm(acc, g_post_ref[...])


def _resident(shape):
    return pl.BlockSpec(shape, lambda *_: (0,) * len(shape), pipeline_mode=pl.Buffered(1))


def _mixer(x, g_pre, w_in, pool_w, pool_scale, w_pool_out, conv_w, conv_b, ln_g, ln_b,
           w_conv_out, w_o, g_post):
    batch, seq, d = x.shape
    tile = MIXER_TILE
    assert seq % tile == 0 and d == D_MODEL
    x_spec = pl.BlockSpec((None, tile, d), lambda b, t: (b, t, 0))
    operands = (g_pre, w_in, pool_w, pool_scale, w_pool_out, conv_w, conv_b, ln_g, ln_b,
                w_conv_out, w_o, g_post)
    return pl.pallas_call(
        functools.partial(_mixer_kernel, tile=tile),
        grid=(batch, seq // tile),
        in_specs=[x_spec] + [_resident(a.shape) for a in operands],
        out_specs=x_spec,
        out_shape=jax.ShapeDtypeStruct(x.shape, x.dtype),
        scratch_shapes=[
            pltpu.VMEM((tile, d), jnp.bfloat16),
            pltpu.VMEM((POOL_HALO + tile, POOL_WIDTH), jnp.float32),
            pltpu.VMEM((tile, POOL_WIDTH), jnp.bfloat16),
            pltpu.VMEM((tile, 2 * CONV_WIDTH), jnp.float32),
            pltpu.VMEM((SUBLANES, CONV_HALO + tile, CONV_WIDTH), jnp.float32),
            pltpu.VMEM((tile, CONV_WIDTH), jnp.bfloat16),
            pltpu.VMEM((tile, 2 * d), jnp.float32),
            pltpu.VMEM((tile, d), jnp.float32),
            pltpu.VMEM((tile, d), jnp.bfloat16),
        ],
        compiler_params=pltpu.CompilerParams(
            dimension_semantics=("arbitrary", "arbitrary"),
            vmem_limit_bytes=VMEM_LIMIT_BYTES),
        name="mixer",
    )(x, *operands)


def _mlp(x, g_pre, w_up, w_down, g_post):
    tokens, d = x.shape
    tile = MLP_TILE
    assert tokens % tile == 0 and d == D_MODEL
    x_spec = pl.BlockSpec((tile, d), lambda i: (i, 0))
    operands = (g_pre, w_up, w_down, g_post)
    return pl.pallas_call(
        _mlp_kernel,
        grid=(tokens // tile,),
        in_specs=[x_spec] + [_resident(a.shape) for a in operands],
        out_specs=x_spec,
        out_shape=jax.ShapeDtypeStruct(x.shape, x.dtype),
        compiler_params=pltpu.CompilerParams(
            dimension_semantics=("arbitrary",),
            vmem_limit_bytes=VMEM_LIMIT_BYTES),
        name="mlp",
    )(x, *operands)


def kernel(x, norm_mix_pre, w_in, pool_w, pool_scale, w_pool_out, conv_w, conv_b, conv_ln_g,
           conv_ln_b, w_conv_out, w_o, norm_mix_post, norm_mlp_pre, w_up, w_down, norm_mlp_post):
    batch, seq, d = x.shape
    bf16 = lambda w: w.astype(jnp.bfloat16)
    row = lambda p: p.reshape(1, -1)
    for l in range(w_in.shape[0]):
        x = _mixer(x, row(norm_mix_pre[l]), bf16(w_in[l]), bf16(pool_w[l]), row(pool_scale[l]),
                   bf16(w_pool_out[l]), conv_w[l], row(conv_b[l]), row(conv_ln_g[l]),
                   row(conv_ln_b[l]), bf16(w_conv_out[l]), bf16(w_o[l]), row(norm_mix_post[l]))
        x = _mlp(x.reshape(batch * seq, d), row(norm_mlp_pre[l]), bf16(w_up[l]), bf16(w_down[l]),
                 row(norm_mlp_post[l])).reshape(batch, seq, d)
    return x
```

```python
import functools

import jax
import jax.numpy as jnp
from jax import lax
from jax.experimental import pallas as pl
from jax.experimental.pallas import tpu as pltpu

D_MODEL = 1024
POOL_WIDTH = 512
POOL_GROUP_DIM = 128
POOL_WINDOWS = (2, 4, 8, 16)
CONV_WIDTH = 512
CONV_K = 31
D_FF = 4 * D_MODEL
EPS = 1e-6

SUBLANES = 8
LANES = 128
POOL_HALO = 16
CONV_HALO = 32
GLU_OFF = POOL_WIDTH
GATE_A_OFF = POOL_WIDTH + 2 * CONV_WIDTH

MIXER_TILE = 512
MLP_TILE = 1024
FF_CHUNK = 1024
PROJ_COLS = 512
CONV_ROWS = 64
PACKED_ROWS = 16
VMEM_LIMIT_BYTES = 56 * 1024 * 1024


def _dot(a, b):
    return jnp.dot(a, b, preferred_element_type=jnp.float32)


def _rmsnorm(x, g):
    ms = jnp.mean(x * x, axis=-1, keepdims=True)
    return x * lax.rsqrt(ms + EPS) * g


def _mixer_kernel(x_ref, g_pre_ref, w_in_ref, pool_w_ref, pool_scale_ref, w_pool_out_ref,
                  conv_w_ref, conv_b_ref, ln_g_ref, ln_b_ref, w_conv_out_ref, w_o_ref,
                  g_post_ref, o_ref,
                  h_ref, pool_ext, mixed_ref, glu_ref, v_ext, conv_x, y_ref, gate_ref, a_ref,
                  merged_ref,
                  *, tile):
    t = pl.program_id(1)
    conv_rows = CONV_HALO + tile

    @pl.when(t == 0)
    def _():
        pool_ext[0:POOL_HALO, :] = jnp.zeros((POOL_HALO, POOL_WIDTH), jnp.float32)
        v_ext[0:CONV_HALO, :] = jnp.zeros((CONV_HALO, CONV_WIDTH), jnp.float32)

    def project(c0):
        return _dot(h_ref[...], w_in_ref[:, c0:c0 + PROJ_COLS])

    h_ref[...] = _rmsnorm(x_ref[...], g_pre_ref[...]).astype(jnp.bfloat16)

    glu_ref[:, 0:CONV_WIDTH] = project(GLU_OFF)
    glu_ref[:, CONV_WIDTH:2 * CONV_WIDTH] = project(GLU_OFF + CONV_WIDTH)
    for r0 in range(0, tile, CONV_ROWS):
        rows = slice(r0, r0 + CONV_ROWS)
        v_ext[CONV_HALO + r0:CONV_HALO + r0 + CONV_ROWS, :] = (
            glu_ref[rows, 0:CONV_WIDTH]
            * jax.nn.sigmoid(glu_ref[rows, CONV_WIDTH:2 * CONV_WIDTH]))
        if r0 == CONV_ROWS:
            pool_ext[POOL_HALO:POOL_HALO + tile, :] = project(0)

    for i, c0 in enumerate(range(0, CONV_WIDTH, LANES)):
        cols = slice(c0, c0 + LANES)
        x0 = v_ext[:, cols]
        for s in range(SUBLANES):
            r = x0 if s == 0 else pltpu.roll(x0, conv_rows - s, axis=0)
            conv_x[s, :, cols] = r.astype(jnp.bfloat16)
            conv_x[s + SUBLANES, :, cols] = pltpu.roll(
                r, conv_rows - SUBLANES, axis=0).astype(jnp.bfloat16)
        gate_ref[:, i * PROJ_COLS:(i + 1) * PROJ_COLS] = project(GATE_A_OFF + i * PROJ_COLS)

    frame = t * tile + lax.broadcasted_iota(jnp.int32, (tile, 1), 0)
    for g, window in enumerate(POOL_WINDOWS):
        cols = slice(g * POOL_GROUP_DIM, (g + 1) * POOL_GROUP_DIM)
        ext = pool_ext[:, cols]
        s = ext
        step = 1
        while step < window:
            s = s + pltpu.roll(s, step, axis=0)
            step *= 2
        inv_count = 1.0 / jnp.minimum(frame + 1, window).astype(jnp.float32)
        pooled = s[POOL_HALO:, :] * inv_count - ext[POOL_HALO:, :]
        mixed_g = _dot(pooled.astype(jnp.bfloat16), pool_w_ref[g]) * pool_scale_ref[:, cols]
        mixed_ref[:, cols] = mixed_g.astype(jnp.bfloat16)

    first_tap = CONV_HALO - (CONV_K - 1)
    for r0 in range(0, tile, CONV_ROWS):
        accs = []
        for c0 in range(0, CONV_WIDTH, LANES):
            cols = slice(c0, c0 + LANES)
            acc = None
            for k in range(CONV_K):
                q, s = divmod(first_tap + k, PACKED_ROWS)
                lo = r0 + q * PACKED_ROWS
                w_k = jnp.concatenate([conv_w_ref[k, :, cols]] * (CONV_ROWS // PACKED_ROWS), axis=0)
                term = w_k * conv_x[s, lo:lo + CONV_ROWS, cols]
                acc = term if acc is None else acc + term
            accs.append(acc.astype(jnp.float32) + conv_b_ref[:, cols])
        c = jnp.concatenate(accs, axis=-1)
        mu = jnp.mean(c, axis=-1, keepdims=True)
        cc = c - mu
        var = jnp.mean(cc * cc, axis=-1, keepdims=True)
        y = cc * lax.rsqrt(var + EPS) * ln_g_ref[...] + ln_b_ref[...]
        y_ref[r0:r0 + CONV_ROWS, :] = (y * jax.nn.sigmoid(y)).astype(jnp.bfloat16)
        if r0 // CONV_ROWS in (1, 3):
            c0 = (r0 // CONV_ROWS // 2) * PROJ_COLS
            a_ref[:, c0:c0 + PROJ_COLS] = _dot(mixed_ref[...], w_pool_out_ref[:, c0:c0 + PROJ_COLS])

    for c0 in range(0, D_MODEL, PROJ_COLS):
        cols = slice(c0, c0 + PROJ_COLS)
        b = _dot(y_ref[...], w_conv_out_ref[:, cols])
        merged = (jax.nn.sigmoid(gate_ref[:, cols]) * a_ref[:, cols]
                  + jax.nn.sigmoid(gate_ref[:, D_MODEL + c0:D_MODEL + c0 + PROJ_COLS]) * b)
        merged_ref[:, cols] = merged.astype(jnp.bfloat16)
    mix = _dot(merged_ref[...], w_o_ref[...])
    o_ref[...] = x_ref[...] + _rmsnorm(mix, g_post_ref[...])

    pool_ext[0:POOL_HALO, :] = pool_ext[tile:tile + POOL_HALO, :]
    v_ext[0:CONV_HALO, :] = v_ext[tile:tile + CONV_HALO, :]


def _mlp_kernel(x_ref, g_pre_ref, w_up_ref, w_down_ref, g_post_ref, o_ref):
    x = x_ref[...]
    h = _rmsnorm(x, g_pre_ref[...]).astype(jnp.bfloat16)
    acc = None
    for c0 in range(0, D_FF, FF_CHUNK):
        a = jnp.maximum(_dot(h, w_up_ref[:, c0:c0 + FF_CHUNK]), 0.0)
        p = _dot((a * a).astype(jnp.bfloat16), w_down_ref[c0:c0 + FF_CHUNK, :])
        acc = p if acc is None else acc + p
    o_ref[...] = x + _rmsnorm(acc, g_post_ref[...])


def _resident(shape):
    return pl.BlockSpec(shape, lambda *_: (0,) * len(shape), pipeline_mode=pl.Buffered(1))


def _mixer(x, g_pre, w_in, pool_w, pool_scale, w_pool_out, conv_w, conv_b, ln_g, ln_b,
           w_conv_out, w_o, g_post):
    batch, seq, d = x.shape
    tile = MIXER_TILE
    assert seq % tile == 0 and d == D_MODEL
    x_spec = pl.BlockSpec((None, tile, d), lambda b, t: (b, t, 0))
    operands = (g_pre, w_in, pool_w, pool_scale, w_pool_out, conv_w, conv_b, ln_g, ln_b,
                w_conv_out, w_o, g_post)
    return pl.pallas_call(
        functools.partial(_mixer_kernel, tile=tile),
        grid=(batch, seq // tile),
        in_specs=[x_spec] + [_resident(a.shape) for a in operands],
        out_specs=x_spec,
        out_shape=jax.ShapeDtypeStruct(x.shape, x.dtype),
        scratch_shapes=[
            pltpu.VMEM((tile, d), jnp.bfloat16),
            pltpu.VMEM((POOL_HALO + tile, POOL_WIDTH), jnp.float32),
            pltpu.VMEM((tile, POOL_WIDTH), jnp.bfloat16),
            pltpu.VMEM((tile, 2 * CONV_WIDTH), jnp.float32),
            pltpu.VMEM((CONV_HALO + tile, CONV_WIDTH), jnp.float32),
            pltpu.VMEM((PACKED_ROWS, CONV_HALO + tile, CONV_WIDTH), jnp.bfloat16),
            pltpu.VMEM((tile, CONV_WIDTH), jnp.bfloat16),
            pltpu.VMEM((tile, 2 * d), jnp.float32),
            pltpu.VMEM((tile, d), jnp.float32),
            pltpu.VMEM((tile, d), jnp.bfloat16),
        ],
        compiler_params=pltpu.CompilerParams(
            dimension_semantics=("arbitrary", "arbitrary"),
            vmem_limit_bytes=VMEM_LIMIT_BYTES),
        name="mixer",
    )(x, *operands)


def _mlp(x, g_pre, w_up, w_down, g_post):
    tokens, d = x.shape
    tile = MLP_TILE
    assert tokens % tile == 0 and d == D_MODEL
    x_spec = pl.BlockSpec((tile, d), lambda i: (i, 0))
    operands = (g_pre, w_up, w_down, g_post)
    return pl.pallas_call(
        _mlp_kernel,
        grid=(tokens // tile,),
        in_specs=[x_spec] + [_resident(a.shape) for a in operands],
        out_specs=x_spec,
        out_shape=jax.ShapeDtypeStruct(x.shape, x.dtype),
        compiler_params=pltpu.CompilerParams(
            dimension_semantics=("arbitrary",),
            vmem_limit_bytes=VMEM_LIMIT_BYTES),
        name="mlp",
    )(x, *operands)


def kernel(x, norm_mix_pre, w_in, pool_w, pool_scale, w_pool_out, conv_w, conv_b, conv_ln_g,
           conv_ln_b, w_conv_out, w_o, norm_mix_post, norm_mlp_pre, w_up, w_down, norm_mlp_post):
    batch, seq, d = x.shape
    bf16 = lambda w: w.astype(jnp.bfloat16)
    row = lambda p: p.reshape(1, -1)
    packed_rows = lambda w: jnp.broadcast_to(w[:, None, :], (w.shape[0], PACKED_ROWS, w.shape[1]))
    for l in range(w_in.shape[0]):
        x = _mixer(x, row(norm_mix_pre[l]), bf16(w_in[l]), bf16(pool_w[l]), row(pool_scale[l]),
                   bf16(w_pool_out[l]), packed_rows(bf16(conv_w[l])), row(conv_b[l]),
                   row(conv_ln_g[l]),
                   row(conv_ln_b[l]), bf16(w_conv_out[l]), bf16(w_o[l]), row(norm_mix_post[l]))
        x = _mlp(x.reshape(batch * seq, d), row(norm_mlp_pre[l]), bf16(w_up[l]), bf16(w_down[l]),
                 row(norm_mlp_post[l])).reshape(batch, seq, d)
    return x
```

```python
import functools

import jax
import jax.numpy as jnp
from jax import lax
from jax.experimental import pallas as pl
from jax.experimental.pallas import tpu as pltpu

D_MODEL = 1024
POOL_WIDTH = 512
POOL_GROUP_DIM = 128
POOL_WINDOWS = (2, 4, 8, 16)
CONV_WIDTH = 512
CONV_K = 31
D_FF = 4 * D_MODEL
EPS = 1e-6

SUBLANES = 8
LANES = 128
POOL_HALO = 16
CONV_HALO = 32
GLU_OFF = POOL_WIDTH
GATE_A_OFF = POOL_WIDTH + 2 * CONV_WIDTH

MIXER_TILE = 512
MLP_TILE = 1024
FF_CHUNK = 1024
PROJ_COLS = 512
CONV_ROWS = 64
VMEM_LIMIT_BYTES = 56 * 1024 * 1024


def _dot(a, b):
    return jnp.dot(a, b, preferred_element_type=jnp.float32)


def _rmsnorm(x, g):
    ms = jnp.mean(x * x, axis=-1, keepdims=True)
    return x * lax.rsqrt(ms + EPS) * g


def _fold_pool_kernel(pool_w_ref, pool_scale_ref, w_pool_out_ref, o_ref):
    scaled = pool_w_ref[...] * pool_scale_ref[...]
    o_ref[...] = jnp.dot(scaled, w_pool_out_ref[...], precision=lax.Precision.HIGHEST,
                         preferred_element_type=jnp.float32).astype(o_ref.dtype)


def _fold_pool(pool_w, pool_scale, w_pool_out):
    groups, gdim, _ = pool_w.shape
    d = w_pool_out.shape[-1]
    out = pl.pallas_call(
        _fold_pool_kernel,
        grid=(groups,),
        in_specs=[pl.BlockSpec((None, gdim, gdim), lambda g: (g, 0, 0)),
                  pl.BlockSpec((None, 1, gdim), lambda g: (g, 0, 0)),
                  pl.BlockSpec((None, gdim, d), lambda g: (g, 0, 0))],
        out_specs=pl.BlockSpec((None, gdim, d), lambda g: (g, 0, 0)),
        out_shape=jax.ShapeDtypeStruct((groups, gdim, d), jnp.bfloat16),
        name="fold_pool",
    )(pool_w, pool_scale.reshape(groups, 1, gdim), w_pool_out.reshape(groups, gdim, d))
    return out.reshape(groups * gdim, d)


def _mixer_kernel(x_ref, g_pre_ref, w_in_ref, w_pool_ref, conv_w_ref, conv_b_ref, ln_g_ref,
                  ln_b_ref, w_conv_out_ref, w_o_ref, g_post_ref, o_ref,
                  h_ref, pool_ext, pooled_ref, glu_ref, conv_x, y_ref, gate_ref, a_ref, merged_ref,
                  *, tile):
    t = pl.program_id(1)
    conv_rows = CONV_HALO + tile

    @pl.when(t == 0)
    def _():
        pool_ext[0:POOL_HALO, :] = jnp.zeros((POOL_HALO, POOL_WIDTH), jnp.float32)
        conv_x[0, 0:CONV_HALO, :] = jnp.zeros((CONV_HALO, CONV_WIDTH), jnp.float32)

    def project(c0):
        return _dot(h_ref[...], w_in_ref[:, c0:c0 + PROJ_COLS])

    h_ref[...] = _rmsnorm(x_ref[...], g_pre_ref[...]).astype(jnp.bfloat16)

    glu_ref[:, 0:CONV_WIDTH] = project(GLU_OFF)
    glu_ref[:, CONV_WIDTH:2 * CONV_WIDTH] = project(GLU_OFF + CONV_WIDTH)
    for r0 in range(0, tile, CONV_ROWS):
        rows = slice(r0, r0 + CONV_ROWS)
        conv_x[0, CONV_HALO + r0:CONV_HALO + r0 + CONV_ROWS, :] = (
            glu_ref[rows, 0:CONV_WIDTH]
            * jax.nn.sigmoid(glu_ref[rows, CONV_WIDTH:2 * CONV_WIDTH]))
        if r0 == CONV_ROWS:
            pool_ext[POOL_HALO:POOL_HALO + tile, :] = project(0)

    for s in range(1, SUBLANES):
        for c0 in range(0, CONV_WIDTH, LANES):
            cols = slice(c0, c0 + LANES)
            conv_x[s, :, cols] = pltpu.roll(conv_x[0, :, cols], conv_rows - s, axis=0)
        if s % 2 == 1:
            c0 = (s // 2) * PROJ_COLS
            gate_ref[:, c0:c0 + PROJ_COLS] = jax.nn.sigmoid(project(GATE_A_OFF + c0))

    frame = t * tile + lax.broadcasted_iota(jnp.int32, (tile, 1), 0)
    for g, window in enumerate(POOL_WINDOWS):
        cols = slice(g * POOL_GROUP_DIM, (g + 1) * POOL_GROUP_DIM)
        ext = pool_ext[:, cols]
        s = ext
        step = 1
        while step < window:
            s = s + pltpu.roll(s, step, axis=0)
            step *= 2
        inv_count = 1.0 / jnp.minimum(frame + 1, window).astype(jnp.float32)
        pooled = s[POOL_HALO:, :] * inv_count - ext[POOL_HALO:, :]
        pooled_ref[:, cols] = pooled.astype(jnp.bfloat16)

    first_tap = CONV_HALO - (CONV_K - 1)
    for r0 in range(0, tile, CONV_ROWS):
        accs = []
        for c0 in range(0, CONV_WIDTH, LANES):
            cols = slice(c0, c0 + LANES)
            acc = jnp.broadcast_to(conv_b_ref[:, cols], (CONV_ROWS, LANES))
            for k in range(CONV_K):
                q, s = divmod(first_tap + k, SUBLANES)
                lo = r0 + q * SUBLANES
                acc = acc + conv_w_ref[k:k + 1, cols] * conv_x[s, lo:lo + CONV_ROWS, cols]
            accs.append(acc)
        c = jnp.concatenate(accs, axis=-1)
        mu = jnp.mean(c, axis=-1, keepdims=True)
        cc = c - mu
        var = jnp.mean(cc * cc, axis=-1, keepdims=True)
        y = cc * lax.rsqrt(var + EPS) * ln_g_ref[...] + ln_b_ref[...]
        y_ref[r0:r0 + CONV_ROWS, :] = (y * jax.nn.sigmoid(y)).astype(jnp.bfloat16)
        if r0 // CONV_ROWS in (1, 3):
            c0 = (r0 // CONV_ROWS // 2) * PROJ_COLS
            a_ref[:, c0:c0 + PROJ_COLS] = _dot(pooled_ref[...], w_pool_ref[:, c0:c0 + PROJ_COLS])

    for c0 in range(0, D_MODEL, PROJ_COLS):
        cols = slice(c0, c0 + PROJ_COLS)
        b = _dot(y_ref[...], w_conv_out_ref[:, cols])
        merged = (gate_ref[:, cols] * a_ref[:, cols]
                  + gate_ref[:, D_MODEL + c0:D_MODEL + c0 + PROJ_COLS] * b)
        merged_ref[:, cols] = merged.astype(jnp.bfloat16)
    mix = _dot(merged_ref[...], w_o_ref[...])
    o_ref[...] = x_ref[...] + _rmsnorm(mix, g_post_ref[...])

    pool_ext[0:POOL_HALO, :] = pool_ext[tile:tile + POOL_HALO, :]
    conv_x[0, 0:CONV_HALO, :] = conv_x[0, tile:tile + CONV_HALO, :]


def _mlp_kernel(x_ref, g_pre_ref, w_up_ref, w_down_ref, g_post_ref, o_ref):
    x = x_ref[...]
    h = _rmsnorm(x, g_pre_ref[...]).astype(jnp.bfloat16)
    acc = None
    for c0 in range(0, D_FF, FF_CHUNK):
        a = jnp.maximum(_dot(h, w_up_ref[:, c0:c0 + FF_CHUNK]), 0.0)
        p = _dot((a * a).astype(jnp.bfloat16), w_down_ref[c0:c0 + FF_CHUNK, :])
        acc = p if acc is None else acc + p
    o_ref[...] = x + _rmsnorm(acc, g_post_ref[...])


def _resident(shape):
    return pl.BlockSpec(shape, lambda *_: (0,) * len(shape), pipeline_mode=pl.Buffered(1))


def _mixer(x, g_pre, w_in, w_pool, conv_w, conv_b, ln_g, ln_b, w_conv_out, w_o, g_post):
    batch, seq, d = x.shape
    tile = MIXER_TILE
    assert seq % tile == 0 and d == D_MODEL
    x_spec = pl.BlockSpec((None, tile, d), lambda b, t: (b, t, 0))
    operands = (g_pre, w_in, w_pool, conv_w, conv_b, ln_g, ln_b, w_conv_out, w_o, g_post)
    return pl.pallas_call(
        functools.partial(_mixer_kernel, tile=tile),
        grid=(batch, seq // tile),
        in_specs=[x_spec] + [_resident(a.shape) for a in operands],
        out_specs=x_spec,
        out_shape=jax.ShapeDtypeStruct(x.shape, x.dtype),
        scratch_shapes=[
            pltpu.VMEM((tile, d), jnp.bfloat16),
            pltpu.VMEM((POOL_HALO + tile, POOL_WIDTH), jnp.float32),
            pltpu.VMEM((tile, POOL_WIDTH), jnp.bfloat16),
            pltpu.VMEM((tile, 2 * CONV_WIDTH), jnp.float32),
            pltpu.VMEM((SUBLANES, CONV_HALO + tile, CONV_WIDTH), jnp.float32),
            pltpu.VMEM((tile, CONV_WIDTH), jnp.bfloat16),
            pltpu.VMEM((tile, 2 * d), jnp.float32),
            pltpu.VMEM((tile, d), jnp.float32),
            pltpu.VMEM((tile, d), jnp.bfloat16),
        ],
        compiler_params=pltpu.CompilerParams(
            dimension_semantics=("arbitrary", "arbitrary"),
            vmem_limit_bytes=VMEM_LIMIT_BYTES),
        name="mixer",
    )(x, *operands)


def _mlp(x, g_pre, w_up, w_down, g_post):
    tokens, d = x.shape
    tile = MLP_TILE
    assert tokens % tile == 0 and d == D_MODEL
    x_spec = pl.BlockSpec((tile, d), lambda i: (i, 0))
    operands = (g_pre, w_up, w_down, g_post)
    return pl.pallas_call(
        _mlp_kernel,
        grid=(tokens // tile,),
        in_specs=[x_spec] + [_resident(a.shape) for a in operands],
        out_specs=x_spec,
        out_shape=jax.ShapeDtypeStruct(x.shape, x.dtype),
        compiler_params=pltpu.CompilerParams(
            dimension_semantics=("arbitrary",),
            vmem_limit_bytes=VMEM_LIMIT_BYTES),
        name="mlp",
    )(x, *operands)


def kernel(x, norm_mix_pre, w_in, pool_w, pool_scale, w_pool_out, conv_w, conv_b, conv_ln_g,
           conv_ln_b, w_conv_out, w_o, norm_mix_post, norm_mlp_pre, w_up, w_down, norm_mlp_post):
    batch, seq, d = x.shape
    bf16 = lambda w: w.astype(jnp.bfloat16)
    row = lambda p: p.reshape(1, -1)
    for l in range(w_in.shape[0]):
        w_pool = _fold_pool(pool_w[l], pool_scale[l], w_pool_out[l])
        x = _mixer(x, row(norm_mix_pre[l]), bf16(w_in[l]), w_pool, conv_w[l], row(conv_b[l]),
                   row(conv_ln_g[l]), row(conv_ln_b[l]), bf16(w_conv_out[l]), bf16(w_o[l]),
                   row(norm_mix_post[l]))
        x = _mlp(x.reshape(batch * seq, d), row(norm_mlp_pre[l]), bf16(w_up[l]), bf16(w_down[l]),
                 row(norm_mlp_post[l])).reshape(batch, seq, d)
    return x
```

```python
import functools

import jax
import jax.numpy as jnp
from jax import lax
from jax.experimental import pallas as pl
from jax.experimental.pallas import tpu as pltpu

D_MODEL = 1024
POOL_WIDTH = 512
POOL_GROUP_DIM = 128
POOL_WINDOWS = (2, 4, 8, 16)
CONV_WIDTH = 512
CONV_K = 31
D_FF = 4 * D_MODEL
EPS = 1e-6

SUBLANES = 8
LANES = 128
POOL_HALO = 16
CONV_HALO = 32
GLU_OFF = POOL_WIDTH
GATE_A_OFF = POOL_WIDTH + 2 * CONV_WIDTH

MIXER_TILE = 512
MLP_TILE = 1024
FF_CHUNK = 1024
PROJ_COLS = 512
CONV_ROWS = 64
W_IN_CHUNK_ROWS = 128
W_OUT_CHUNK_ROWS = 256
W_UP_CHUNK_ROWS = 128
W_DOWN_CHUNK_ROWS = 512
VMEM_LIMIT_BYTES = 56 * 1024 * 1024


def _dot(a, b):
    return jnp.dot(a, b, preferred_element_type=jnp.float32)


def _rmsnorm(x, g):
    ms = jnp.mean(x * x, axis=-1, keepdims=True)
    return x * lax.rsqrt(ms + EPS) * g


def _load_as_bf16(src_hbm, dst_ref, stage, sem, chunk_rows):
    n_chunks = src_hbm.shape[0] // chunk_rows

    def copy(c):
        return pltpu.make_async_copy(src_hbm.at[pl.ds(c * chunk_rows, chunk_rows), :],
                                     stage.at[c % 2], sem.at[c % 2])

    copy(0).start()
    for c in range(n_chunks):
        if c + 1 < n_chunks:
            copy(c + 1).start()
        copy(c).wait()
        dst_ref[c * chunk_rows:(c + 1) * chunk_rows, :] = stage[c % 2].astype(jnp.bfloat16)


def _fold_pool_kernel(pool_w_ref, pool_scale_ref, w_pool_out_ref, o_ref):
    scaled = pool_w_ref[...] * pool_scale_ref[...]
    o_ref[...] = jnp.dot(scaled, w_pool_out_ref[...], precision=lax.Precision.HIGHEST,
                         preferred_element_type=jnp.float32).astype(o_ref.dtype)


def _fold_pool(pool_w, pool_scale, w_pool_out):
    groups, gdim, _ = pool_w.shape
    d = w_pool_out.shape[-1]
    out = pl.pallas_call(
        _fold_pool_kernel,
        grid=(groups,),
        in_specs=[pl.BlockSpec((None, gdim, gdim), lambda g: (g, 0, 0)),
                  pl.BlockSpec((None, 1, gdim), lambda g: (g, 0, 0)),
                  pl.BlockSpec((None, gdim, d), lambda g: (g, 0, 0))],
        out_specs=pl.BlockSpec((None, gdim, d), lambda g: (g, 0, 0)),
        out_shape=jax.ShapeDtypeStruct((groups, gdim, d), jnp.bfloat16),
        name="fold_pool",
    )(pool_w, pool_scale.reshape(groups, 1, gdim), w_pool_out.reshape(groups, gdim, d))
    return out.reshape(groups * gdim, d)


def _mixer_kernel(x_ref, g_pre_ref, w_in_hbm, w_pool_ref, conv_w_ref, conv_b_ref, ln_g_ref,
                  ln_b_ref, w_conv_out_hbm, w_o_hbm, g_post_ref, o_ref,
                  w_in_ref, w_conv_out_ref, w_o_ref, stage_in, stage_out, sem,
                  h_ref, pool_ext, pooled_ref, glu_ref, conv_x, y_ref, gate_ref, a_ref, merged_ref,
                  *, tile):
    t = pl.program_id(1)
    conv_rows = CONV_HALO + tile

    @pl.when((pl.program_id(0) == 0) & (t == 0))
    def _():
        _load_as_bf16(w_in_hbm, w_in_ref, stage_in, sem, W_IN_CHUNK_ROWS)
        _load_as_bf16(w_conv_out_hbm, w_conv_out_ref, stage_out, sem, W_OUT_CHUNK_ROWS)
        _load_as_bf16(w_o_hbm, w_o_ref, stage_out, sem, W_OUT_CHUNK_ROWS)

    @pl.when(t == 0)
    def _():
        pool_ext[0:POOL_HALO, :] = jnp.zeros((POOL_HALO, POOL_WIDTH), jnp.float32)
        conv_x[0, 0:CONV_HALO, :] = jnp.zeros((CONV_HALO, CONV_WIDTH), jnp.float32)

    def project(c0):
        return _dot(h_ref[...], w_in_ref[:, c0:c0 + PROJ_COLS])

    h_ref[...] = _rmsnorm(x_ref[...], g_pre_ref[...]).astype(jnp.bfloat16)

    glu_ref[:, 0:CONV_WIDTH] = project(GLU_OFF)
    glu_ref[:, CONV_WIDTH:2 * CONV_WIDTH] = project(GLU_OFF + CONV_WIDTH)
    for r0 in range(0, tile, CONV_ROWS):
        rows = slice(r0, r0 + CONV_ROWS)
        conv_x[0, CONV_HALO + r0:CONV_HALO + r0 + CONV_ROWS, :] = (
            glu_ref[rows, 0:CONV_WIDTH]
            * jax.nn.sigmoid(glu_ref[rows, CONV_WIDTH:2 * CONV_WIDTH]))
        if r0 == CONV_ROWS:
            pool_ext[POOL_HALO:POOL_HALO + tile, :] = project(0)

    for s in range(1, SUBLANES):
        for c0 in range(0, CONV_WIDTH, LANES):
            cols = slice(c0, c0 + LANES)
            conv_x[s, :, cols] = pltpu.roll(conv_x[0, :, cols], conv_rows - s, axis=0)
        if s % 2 == 1:
            c0 = (s // 2) * PROJ_COLS
            gate_ref[:, c0:c0 + PROJ_COLS] = jax.nn.sigmoid(project(GATE_A_OFF + c0))

    frame = t * tile + lax.broadcasted_iota(jnp.int32, (tile, 1), 0)
    for g, window in enumerate(POOL_WINDOWS):
        cols = slice(g * POOL_GROUP_DIM, (g + 1) * POOL_GROUP_DIM)
        ext = pool_ext[:, cols]
        s = ext
        step = 1
        while step < window:
            s = s + pltpu.roll(s, step, axis=0)
            step *= 2
        inv_count = 1.0 / jnp.minimum(frame + 1, window).astype(jnp.float32)
        pooled = s[POOL_HALO:, :] * inv_count - ext[POOL_HALO:, :]
        pooled_ref[:, cols] = pooled.astype(jnp.bfloat16)

    first_tap = CONV_HALO - (CONV_K - 1)
    for r0 in range(0, tile, CONV_ROWS):
        accs = []
        for c0 in range(0, CONV_WIDTH, LANES):
            cols = slice(c0, c0 + LANES)
            acc = jnp.broadcast_to(conv_b_ref[:, cols], (CONV_ROWS, LANES))
            for k in range(CONV_K):
                q, s = divmod(first_tap + k, SUBLANES)
                lo = r0 + q * SUBLANES
                acc = acc + conv_w_ref[k:k + 1, cols] * conv_x[s, lo:lo + CONV_ROWS, cols]
            accs.append(acc)
        c = jnp.concatenate(accs, axis=-1)
        mu = jnp.mean(c, axis=-1, keepdims=True)
        cc = c - mu
        var = jnp.mean(cc * cc, axis=-1, keepdims=True)
        y = cc * lax.rsqrt(var + EPS) * ln_g_ref[...] + ln_b_ref[...]
        y_ref[r0:r0 + CONV_ROWS, :] = (y * jax.nn.sigmoid(y)).astype(jnp.bfloat16)
        if r0 // CONV_ROWS in (1, 3):
            c0 = (r0 // CONV_ROWS // 2) * PROJ_COLS
            a_ref[:, c0:c0 + PROJ_COLS] = _dot(pooled_ref[...], w_pool_ref[:, c0:c0 + PROJ_COLS])

    for c0 in range(0, D_MODEL, PROJ_COLS):
        cols = slice(c0, c0 + PROJ_COLS)
        b = _dot(y_ref[...], w_conv_out_ref[:, cols])
        merged = (gate_ref[:, cols] * a_ref[:, cols]
                  + gate_ref[:, D_MODEL + c0:D_MODEL + c0 + PROJ_COLS] * b)
        merged_ref[:, cols] = merged.astype(jnp.bfloat16)
    mix = _dot(merged_ref[...], w_o_ref[...])
    o_ref[...] = x_ref[...] + _rmsnorm(mix, g_post_ref[...])

    pool_ext[0:POOL_HALO, :] = pool_ext[tile:tile + POOL_HALO, :]
    conv_x[0, 0:CONV_HALO, :] = conv_x[0, tile:tile + CONV_HALO, :]


def _mlp_kernel(x_ref, g_pre_ref, w_up_hbm, w_down_hbm, g_post_ref, o_ref,
                w_up_ref, w_down_ref, stage_up, stage_down, sem):
    @pl.when(pl.program_id(0) == 0)
    def _():
        _load_as_bf16(w_up_hbm, w_up_ref, stage_up, sem, W_UP_CHUNK_ROWS)
        _load_as_bf16(w_down_hbm, w_down_ref, stage_down, sem, W_DOWN_CHUNK_ROWS)

    x = x_ref[...]
    h = _rmsnorm(x, g_pre_ref[...]).astype(jnp.bfloat16)
    acc = None
    for c0 in range(0, D_FF, FF_CHUNK):
        a = jnp.maximum(_dot(h, w_up_ref[:, c0:c0 + FF_CHUNK]), 0.0)
        p = _dot((a * a).astype(jnp.bfloat16), w_down_ref[c0:c0 + FF_CHUNK, :])
        acc = p if acc is None else acc + p
    o_ref[...] = x + _rmsnorm(acc, g_post_ref[...])


def _resident(shape):
    return pl.BlockSpec(shape, lambda *_: (0,) * len(shape), pipeline_mode=pl.Buffered(1))


def _mixer(x, g_pre, w_in, w_pool, conv_w, conv_b, ln_g, ln_b, w_conv_out, w_o, g_post):
    batch, seq, d = x.shape
    tile = MIXER_TILE
    assert seq % tile == 0 and d == D_MODEL
    x_spec = pl.BlockSpec((None, tile, d), lambda b, t: (b, t, 0))
    operands = (g_pre, w_in, w_pool, conv_w, conv_b, ln_g, ln_b, w_conv_out, w_o, g_post)
    in_hbm = (w_in, w_conv_out, w_o)
    hbm = pl.BlockSpec(memory_space=pl.ANY)
    return pl.pallas_call(
        functools.partial(_mixer_kernel, tile=tile),
        grid=(batch, seq // tile),
        in_specs=[x_spec] + [hbm if any(a is w for w in in_hbm) else _resident(a.shape)
                             for a in operands],
        out_specs=x_spec,
        out_shape=jax.ShapeDtypeStruct(x.shape, x.dtype),
        scratch_shapes=[
            pltpu.VMEM(w_in.shape, jnp.bfloat16),
            pltpu.VMEM(w_conv_out.shape, jnp.bfloat16),
            pltpu.VMEM(w_o.shape, jnp.bfloat16),
            pltpu.VMEM((2, W_IN_CHUNK_ROWS, w_in.shape[1]), jnp.float32),
            pltpu.VMEM((2, W_OUT_CHUNK_ROWS, d), jnp.float32),
            pltpu.SemaphoreType.DMA((2,)),
            pltpu.VMEM((tile, d), jnp.bfloat16),
            pltpu.VMEM((POOL_HALO + tile, POOL_WIDTH), jnp.float32),
            pltpu.VMEM((tile, POOL_WIDTH), jnp.bfloat16),
            pltpu.VMEM((tile, 2 * CONV_WIDTH), jnp.float32),
            pltpu.VMEM((SUBLANES, CONV_HALO + tile, CONV_WIDTH), jnp.float32),
            pltpu.VMEM((tile, CONV_WIDTH), jnp.bfloat16),
            pltpu.VMEM((tile, 2 * d), jnp.float32),
            pltpu.VMEM((tile, d), jnp.float32),
            pltpu.VMEM((tile, d), jnp.bfloat16),
        ],
        compiler_params=pltpu.CompilerParams(
            dimension_semantics=("arbitrary", "arbitrary"),
            vmem_limit_bytes=VMEM_LIMIT_BYTES),
        name="mixer",
    )(x, *operands)


def _mlp(x, g_pre, w_up, w_down, g_post):
    tokens, d = x.shape
    tile = MLP_TILE
    assert tokens % tile == 0 and d == D_MODEL
    x_spec = pl.BlockSpec((tile, d), lambda i: (i, 0))
    operands = (g_pre, w_up, w_down, g_post)
    return pl.pallas_call(
        _mlp_kernel,
        grid=(tokens // tile,),
        in_specs=[x_spec, _resident(g_pre.shape), pl.BlockSpec(memory_space=pl.ANY),
                  pl.BlockSpec(memory_space=pl.ANY), _resident(g_post.shape)],
        out_specs=x_spec,
        out_shape=jax.ShapeDtypeStruct(x.shape, x.dtype),
        scratch_shapes=[
            pltpu.VMEM(w_up.shape, jnp.bfloat16),
            pltpu.VMEM(w_down.shape, jnp.bfloat16),
            pltpu.VMEM((2, W_UP_CHUNK_ROWS, w_up.shape[1]), jnp.float32),
            pltpu.VMEM((2, W_DOWN_CHUNK_ROWS, w_down.shape[1]), jnp.float32),
            pltpu.SemaphoreType.DMA((2,)),
        ],
        compiler_params=pltpu.CompilerParams(
            dimension_semantics=("arbitrary",),
            vmem_limit_bytes=VMEM_LIMIT_BYTES),
        name="mlp",
    )(x, *operands)


def kernel(x, norm_mix_pre, w_in, pool_w, pool_scale, w_pool_out, conv_w, conv_b, conv_ln_g,
           conv_ln_b, w_conv_out, w_o, norm_mix_post, norm_mlp_pre, w_up, w_down, norm_mlp_post):
    batch, seq, d = x.shape
    row = lambda p: p.reshape(1, -1)
    for l in range(w_in.shape[0]):
        w_pool = _fold_pool(pool_w[l], pool_scale[l], w_pool_out[l])
        x = _mixer(x, row(norm_mix_pre[l]), w_in[l], w_pool, conv_w[l], row(conv_b[l]),
                   row(conv_ln_g[l]), row(conv_ln_b[l]), w_conv_out[l], w_o[l],
                   row(norm_mix_post[l]))
        x = _mlp(x.reshape(batch * seq, d), row(norm_mlp_pre[l]), w_up[l], w_down[l],
                 row(norm_mlp_post[l])).reshape(batch, seq, d)
    return x
```

```python
import functools

import jax
import jax.numpy as jnp
from jax import lax
from jax.experimental import pallas as pl
from jax.experimental.pallas import tpu as pltpu

D_MODEL = 1024
POOL_WIDTH = 512
POOL_GROUP_DIM = 128
POOL_WINDOWS = (2, 4, 8, 16)
CONV_WIDTH = 512
CONV_K = 31
D_FF = 4 * D_MODEL
EPS = 1e-6

SUBLANES = 8
LANES = 128
POOL_HALO = 16
CONV_HALO = 32
GLU_OFF = POOL_WIDTH
GATE_A_OFF = POOL_WIDTH + 2 * CONV_WIDTH

MIXER_TILE = 512
MLP_TILE = 1024
FF_CHUNK = 1024
PROJ_COLS = 512
CONV_ROWS = 64
W_IN_CHUNK_ROWS = 128
W_OUT_CHUNK_ROWS = 256
W_UP_CHUNK_ROWS = 128
W_DOWN_CHUNK_ROWS = 512
VMEM_LIMIT_BYTES = 56 * 1024 * 1024


def _dot(a, b):
    return jnp.dot(a, b, preferred_element_type=jnp.float32)


def _rms_scale(x):
    ms = jnp.mean(x * x, axis=-1, keepdims=True)
    return x * lax.rsqrt(ms + EPS)


def _rmsnorm(x, g):
    return _rms_scale(x) * g


def _stream_rows(src_hbm, stage, sem, chunk_rows, consume):
    n_chunks = src_hbm.shape[0] // chunk_rows

    def copy(c):
        return pltpu.make_async_copy(src_hbm.at[pl.ds(c * chunk_rows, chunk_rows), :],
                                     stage.at[c % 2], sem.at[c % 2])

    copy(0).start()
    for c in range(n_chunks):
        if c + 1 < n_chunks:
            copy(c + 1).start()
        copy(c).wait()
        consume(c, stage[c % 2])


def _load_as_bf16(src_hbm, dst_ref, stage, sem, chunk_rows, row_gain_ref=None):
    if row_gain_ref is not None:
        gain = jnp.broadcast_to(row_gain_ref[...], (LANES, row_gain_ref.shape[1]))
        gain_col = jnp.transpose(gain)[:, 0:1]

    def cast(c, chunk):
        rows = slice(c * chunk_rows, (c + 1) * chunk_rows)
        if row_gain_ref is not None:
            chunk = chunk * gain_col[rows, :]
        dst_ref[rows, :] = chunk.astype(jnp.bfloat16)

    _stream_rows(src_hbm, stage, sem, chunk_rows, cast)


def _load_folded_pool(w_pool_out_hbm, pool_w_ref, pool_scale_ref, dst_ref, stage, sem, chunk_rows):
    groups_per_chunk = chunk_rows // POOL_GROUP_DIM

    def fold(c, chunk):
        for j in range(groups_per_chunk):
            g = c * groups_per_chunk + j
            scaled = pool_w_ref[g] * pool_scale_ref[:, g * POOL_GROUP_DIM:(g + 1) * POOL_GROUP_DIM]
            rows = chunk[j * POOL_GROUP_DIM:(j + 1) * POOL_GROUP_DIM, :]
            folded = jnp.dot(scaled, rows, precision=lax.Precision.HIGHEST,
                             preferred_element_type=jnp.float32)
            dst_ref[g * POOL_GROUP_DIM:(g + 1) * POOL_GROUP_DIM, :] = folded.astype(jnp.bfloat16)

    _stream_rows(w_pool_out_hbm, stage, sem, chunk_rows, fold)


def _mixer_kernel(x_ref, g_pre_ref, w_in_hbm, pool_w_ref, pool_scale_ref, w_pool_out_hbm,
                  conv_w_ref, conv_b_ref, ln_g_ref, ln_b_ref, w_conv_out_hbm, w_o_hbm,
                  g_post_ref, o_ref,
                  w_in_ref, w_pool_ref, w_conv_out_ref, w_o_ref, stage_in, stage_out, sem,
                  h_ref, pool_ext, pooled_ref, glu_ref, conv_x, y_ref, gate_ref, a_ref, merged_ref,
                  *, tile):
    t = pl.program_id(1)
    conv_rows = CONV_HALO + tile

    @pl.when((pl.program_id(0) == 0) & (t == 0))
    def _():
        _load_as_bf16(w_in_hbm, w_in_ref, stage_in, sem, W_IN_CHUNK_ROWS, row_gain_ref=g_pre_ref)
        _load_folded_pool(w_pool_out_hbm, pool_w_ref, pool_scale_ref, w_pool_ref, stage_out, sem,
                          W_OUT_CHUNK_ROWS)
        _load_as_bf16(w_conv_out_hbm, w_conv_out_ref, stage_out, sem, W_OUT_CHUNK_ROWS)
        _load_as_bf16(w_o_hbm, w_o_ref, stage_out, sem, W_OUT_CHUNK_ROWS)

    @pl.when(t == 0)
    def _():
        pool_ext[0:POOL_HALO, :] = jnp.zeros((POOL_HALO, POOL_WIDTH), jnp.float32)
        conv_x[0, 0:CONV_HALO, :] = jnp.zeros((CONV_HALO, CONV_WIDTH), jnp.float32)

    def project(c0):
        return _dot(h_ref[...], w_in_ref[:, c0:c0 + PROJ_COLS])

    h_ref[...] = _rms_scale(x_ref[...]).astype(jnp.bfloat16)

    glu_ref[:, 0:CONV_WIDTH] = project(GLU_OFF)
    glu_ref[:, CONV_WIDTH:2 * CONV_WIDTH] = project(GLU_OFF + CONV_WIDTH)
    for r0 in range(0, tile, CONV_ROWS):
        rows = slice(r0, r0 + CONV_ROWS)
        conv_x[0, CONV_HALO + r0:CONV_HALO + r0 + CONV_ROWS, :] = (
            glu_ref[rows, 0:CONV_WIDTH]
            * jax.nn.sigmoid(glu_ref[rows, CONV_WIDTH:2 * CONV_WIDTH]))
        if r0 == CONV_ROWS:
            pool_ext[POOL_HALO:POOL_HALO + tile, :] = project(0)

    for s in range(1, SUBLANES):
        for c0 in range(0, CONV_WIDTH, LANES):
            cols = slice(c0, c0 + LANES)
            conv_x[s, :, cols] = pltpu.roll(conv_x[0, :, cols], conv_rows - s, axis=0)
        if s % 2 == 1:
            c0 = (s // 2) * PROJ_COLS
            gate_ref[:, c0:c0 + PROJ_COLS] = jax.nn.sigmoid(project(GATE_A_OFF + c0))

    frame = t * tile + lax.broadcasted_iota(jnp.int32, (tile, 1), 0)
    for g, window in enumerate(POOL_WINDOWS):
        cols = slice(g * POOL_GROUP_DIM, (g + 1) * POOL_GROUP_DIM)
        ext = pool_ext[:, cols]
        s = ext
        step = 1
        while step < window:
            s = s + pltpu.roll(s, step, axis=0)
            step *= 2
        inv_count = 1.0 / jnp.minimum(frame + 1, window).astype(jnp.float32)
        pooled = s[POOL_HALO:, :] * inv_count - ext[POOL_HALO:, :]
        pooled_ref[:, cols] = pooled.astype(jnp.bfloat16)

    first_tap = CONV_HALO - (CONV_K - 1)
    for r0 in range(0, tile, CONV_ROWS):
        accs = []
        for c0 in range(0, CONV_WIDTH, LANES):
            cols = slice(c0, c0 + LANES)
            acc = jnp.broadcast_to(conv_b_ref[:, cols], (CONV_ROWS, LANES))
            for k in range(CONV_K):
                q, s = divmod(first_tap + k, SUBLANES)
                lo = r0 + q * SUBLANES
                acc = acc + conv_w_ref[k:k + 1, cols] * conv_x[s, lo:lo + CONV_ROWS, cols]
            accs.append(acc)
        c = jnp.concatenate(accs, axis=-1)
        mu = jnp.mean(c, axis=-1, keepdims=True)
        cc = c - mu
        var = jnp.mean(cc * cc, axis=-1, keepdims=True)
        y = cc * lax.rsqrt(var + EPS) * ln_g_ref[...] + ln_b_ref[...]
        y_ref[r0:r0 + CONV_ROWS, :] = (y * jax.nn.sigmoid(y)).astype(jnp.bfloat16)
        if r0 // CONV_ROWS in (1, 3):
            c0 = (r0 // CONV_ROWS // 2) * PROJ_COLS
            a_ref[:, c0:c0 + PROJ_COLS] = _dot(pooled_ref[...], w_pool_ref[:, c0:c0 + PROJ_COLS])

    for c0 in range(0, D_MODEL, PROJ_COLS):
        cols = slice(c0, c0 + PROJ_COLS)
        b = _dot(y_ref[...], w_conv_out_ref[:, cols])
        merged = (gate_ref[:, cols] * a_ref[:, cols]
                  + gate_ref[:, D_MODEL + c0:D_MODEL + c0 + PROJ_COLS] * b)
        merged_ref[:, cols] = merged.astype(jnp.bfloat16)
    mix = _dot(merged_ref[...], w_o_ref[...])
    o_ref[...] = x_ref[...] + _rmsnorm(mix, g_post_ref[...])

    pool_ext[0:POOL_HALO, :] = pool_ext[tile:tile + POOL_HALO, :]
    conv_x[0, 0:CONV_HALO, :] = conv_x[0, tile:tile + CONV_HALO, :]


def _mlp_kernel(x_ref, g_pre_ref, w_up_hbm, w_down_hbm, g_post_ref, o_ref,
                w_up_ref, w_down_ref, stage_up, stage_down, sem):
    @pl.when(pl.program_id(0) == 0)
    def _():
        _load_as_bf16(w_up_hbm, w_up_ref, stage_up, sem, W_UP_CHUNK_ROWS, row_gain_ref=g_pre_ref)
        _load_as_bf16(w_down_hbm, w_down_ref, stage_down, sem, W_DOWN_CHUNK_ROWS)

    x = x_ref[...]
    h = _rms_scale(x).astype(jnp.bfloat16)
    acc = None
    for c0 in range(0, D_FF, FF_CHUNK):
        a = jnp.maximum(_dot(h, w_up_ref[:, c0:c0 + FF_CHUNK]), 0.0)
        p = _dot((a * a).astype(jnp.bfloat16), w_down_ref[c0:c0 + FF_CHUNK, :])
        acc = p if acc is None else acc + p
    o_ref[...] = x + _rmsnorm(acc, g_post_ref[...])


def _resident(shape):
    return pl.BlockSpec(shape, lambda *_: (0,) * len(shape), pipeline_mode=pl.Buffered(1))


def _mixer(x, g_pre, w_in, pool_w, pool_scale, w_pool_out, conv_w, layer, conv_b, ln_g, ln_b,
           w_conv_out, w_o, g_post):
    batch, seq, d = x.shape
    tile = MIXER_TILE
    assert seq % tile == 0 and d == D_MODEL
    x_spec = pl.BlockSpec((None, tile, d), lambda b, t: (b, t, 0))
    operands = (g_pre, w_in, pool_w, pool_scale, w_pool_out, conv_w, conv_b, ln_g, ln_b,
                w_conv_out, w_o, g_post)
    in_hbm = (w_in, w_pool_out, w_conv_out, w_o)
    hbm = pl.BlockSpec(memory_space=pl.ANY)
    conv_w_spec = pl.BlockSpec((None,) + conv_w.shape[1:], lambda *_: (layer, 0, 0),
                               pipeline_mode=pl.Buffered(1))

    def spec(a):
        if a is conv_w:
            return conv_w_spec
        return hbm if any(a is w for w in in_hbm) else _resident(a.shape)

    return pl.pallas_call(
        functools.partial(_mixer_kernel, tile=tile),
        grid=(batch, seq // tile),
        in_specs=[x_spec] + [spec(a) for a in operands],
        out_specs=x_spec,
        out_shape=jax.ShapeDtypeStruct(x.shape, x.dtype),
        scratch_shapes=[
            pltpu.VMEM(w_in.shape, jnp.bfloat16),
            pltpu.VMEM(w_pool_out.shape, jnp.bfloat16),
            pltpu.VMEM(w_conv_out.shape, jnp.bfloat16),
            pltpu.VMEM(w_o.shape, jnp.bfloat16),
            pltpu.VMEM((2, W_IN_CHUNK_ROWS, w_in.shape[1]), jnp.float32),
            pltpu.VMEM((2, W_OUT_CHUNK_ROWS, d), jnp.float32),
            pltpu.SemaphoreType.DMA((2,)),
            pltpu.VMEM((tile, d), jnp.bfloat16),
            pltpu.VMEM((POOL_HALO + tile, POOL_WIDTH), jnp.float32),
            pltpu.VMEM((tile, POOL_WIDTH), jnp.bfloat16),
            pltpu.VMEM((tile, 2 * CONV_WIDTH), jnp.float32),
            pltpu.VMEM((SUBLANES, CONV_HALO + tile, CONV_WIDTH), jnp.float32),
            pltpu.VMEM((tile, CONV_WIDTH), jnp.bfloat16),
            pltpu.VMEM((tile, 2 * d), jnp.float32),
            pltpu.VMEM((tile, d), jnp.float32),
            pltpu.VMEM((tile, d), jnp.bfloat16),
        ],
        compiler_params=pltpu.CompilerParams(
            dimension_semantics=("arbitrary", "arbitrary"),
            vmem_limit_bytes=VMEM_LIMIT_BYTES),
        name="mixer",
    )(x, *operands)


def _mlp(x, g_pre, w_up, w_down, g_post):
    tokens, d = x.shape
    tile = MLP_TILE
    assert tokens % tile == 0 and d == D_MODEL
    x_spec = pl.BlockSpec((tile, d), lambda i: (i, 0))
    operands = (g_pre, w_up, w_down, g_post)
    return pl.pallas_call(
        _mlp_kernel,
        grid=(tokens // tile,),
        in_specs=[x_spec, _resident(g_pre.shape), pl.BlockSpec(memory_space=pl.ANY),
                  pl.BlockSpec(memory_space=pl.ANY), _resident(g_post.shape)],
        out_specs=x_spec,
        out_shape=jax.ShapeDtypeStruct(x.shape, x.dtype),
        scratch_shapes=[
            pltpu.VMEM(w_up.shape, jnp.bfloat16),
            pltpu.VMEM(w_down.shape, jnp.bfloat16),
            pltpu.VMEM((2, W_UP_CHUNK_ROWS, w_up.shape[1]), jnp.float32),
            pltpu.VMEM((2, W_DOWN_CHUNK_ROWS, w_down.shape[1]), jnp.float32),
            pltpu.SemaphoreType.DMA((2,)),
        ],
        compiler_params=pltpu.CompilerParams(
            dimension_semantics=("arbitrary",),
            vmem_limit_bytes=VMEM_LIMIT_BYTES),
        name="mlp",
    )(x, *operands)


def kernel(x, norm_mix_pre, w_in, pool_w, pool_scale, w_pool_out, conv_w, conv_b, conv_ln_g,
           conv_ln_b, w_conv_out, w_o, norm_mix_post, norm_mlp_pre, w_up, w_down, norm_mlp_post):
    batch, seq, d = x.shape
    row = lambda p: p.reshape(1, -1)
    for l in range(w_in.shape[0]):
        x = _mixer(x, row(norm_mix_pre[l]), w_in[l], pool_w[l],
                   row(pool_scale[l]), w_pool_out[l],
                   conv_w, l, row(conv_b[l]), row(conv_ln_g[l]), row(conv_ln_b[l]), w_conv_out[l],
                   w_o[l], row(norm_mix_post[l]))
        x = _mlp(x.reshape(batch * seq, d), row(norm_mlp_pre[l]), w_up[l], w_down[l],
                 row(norm_mlp_post[l])).reshape(batch, seq, d)
    return x
```

```python
import functools

import jax
import jax.numpy as jnp
from jax import lax
from jax.experimental import pallas as pl
from jax.experimental.pallas import tpu as pltpu

D_MODEL = 1024
POOL_WIDTH = 512
POOL_GROUP_DIM = 128
POOL_WINDOWS = (2, 4, 8, 16)
CONV_WIDTH = 512
CONV_K = 31
D_FF = 4 * D_MODEL
EPS = 1e-6

SUBLANES = 8
LANES = 128
POOL_HALO = 16
CONV_HALO = 32
GLU_OFF = POOL_WIDTH
GATE_A_OFF = POOL_WIDTH + 2 * CONV_WIDTH

MIXER_TILE = 512
MLP_TILE = 1024
FF_CHUNK = 1024
PROJ_COLS = 512
CONV_ROWS = 64
PACKED_ROWS = 16
W_IN_CHUNK_ROWS = 128
W_OUT_CHUNK_ROWS = 256
W_UP_CHUNK_ROWS = 128
W_DOWN_CHUNK_ROWS = 512
VMEM_LIMIT_BYTES = 56 * 1024 * 1024


def _dot(a, b):
    return jnp.dot(a, b, preferred_element_type=jnp.float32)


def _rmsnorm(x, g):
    ms = jnp.mean(x * x, axis=-1, keepdims=True)
    return x * lax.rsqrt(ms + EPS) * g


def _zero_after(v):
    return jnp.minimum(jnp.abs(v), 0.0)


def _repeat(v, rows, cols):
    v = jnp.concatenate([v] * (cols // v.shape[1]), axis=1)
    return jnp.concatenate([v] * (rows // v.shape[0]), axis=0)


def _stream_rows(src_hbm, stage, sem, chunk_rows, consume):
    n_chunks = src_hbm.shape[0] // chunk_rows

    def copy(c):
        return pltpu.make_async_copy(src_hbm.at[pl.ds(c * chunk_rows, chunk_rows), :],
                                     stage.at[c % 2], sem.at[c % 2])

    copy(0).start()
    for c in range(n_chunks):
        if c + 1 < n_chunks:
            copy(c + 1).start()
        copy(c).wait()
        consume(c, stage[c % 2])


def _load_as_bf16(src_hbm, dst_ref, stage, sem, chunk_rows):
    def cast(c, chunk):
        dst_ref[c * chunk_rows:(c + 1) * chunk_rows, :] = chunk.astype(jnp.bfloat16)

    _stream_rows(src_hbm, stage, sem, chunk_rows, cast)


def _load_folded_pool(w_pool_out_hbm, pool_w_ref, pool_scale_ref, dst_ref, stage, sem, chunk_rows):
    groups_per_chunk = chunk_rows // POOL_GROUP_DIM

    def fold(c, chunk):
        for j in range(groups_per_chunk):
            g = c * groups_per_chunk + j
            scaled = pool_w_ref[g] * pool_scale_ref[g]
            rows = chunk[j * POOL_GROUP_DIM:(j + 1) * POOL_GROUP_DIM, :]
            folded = jnp.dot(scaled, rows, precision=lax.Precision.HIGHEST,
                             preferred_element_type=jnp.float32)
            dst_ref[g * POOL_GROUP_DIM:(g + 1) * POOL_GROUP_DIM, :] = folded.astype(jnp.bfloat16)

    _stream_rows(w_pool_out_hbm, stage, sem, chunk_rows, fold)


def _mixer_kernel(x_ref, g_pre_ref, w_in_hbm, pool_w_ref, pool_scale_ref, w_pool_out_hbm,
                  conv_w_ref, conv_b_ref, ln_g_ref, ln_b_ref, w_conv_out_hbm, w_o_hbm,
                  g_post_ref, o_ref,
                  w_in_ref, w_pool_ref, w_conv_out_ref, w_o_ref, stage_in, stage_out, sem,
                  h_ref, pool_ext, pooled_ref, glu_ref, conv_x, conv_out, y_ref, gate_ref, merged_ref,
                  *, tile):
    t = pl.program_id(1)
    conv_rows = CONV_HALO + tile

    @pl.when((pl.program_id(0) == 0) & (t == 0))
    def _():
        _load_as_bf16(w_in_hbm, w_in_ref, stage_in, sem, W_IN_CHUNK_ROWS)
        _load_folded_pool(w_pool_out_hbm, pool_w_ref, pool_scale_ref, w_pool_ref, stage_out, sem,
                          W_OUT_CHUNK_ROWS)
        _load_as_bf16(w_conv_out_hbm, w_conv_out_ref, stage_out, sem, W_OUT_CHUNK_ROWS)
        _load_as_bf16(w_o_hbm, w_o_ref, stage_out, sem, W_OUT_CHUNK_ROWS)

    @pl.when(t == 0)
    def _():
        pool_ext[0:POOL_HALO, :] = jnp.zeros((POOL_HALO, POOL_WIDTH), jnp.float32)
        conv_x[0, 0:CONV_HALO, :] = jnp.zeros((CONV_HALO, CONV_WIDTH), jnp.float32)

    def project(c0):
        return _dot(h_ref[...], w_in_ref[:, c0:c0 + PROJ_COLS])

    h_ref[...] = _rmsnorm(x_ref[...], g_pre_ref[...]).astype(jnp.bfloat16)

    glu_ref[:, 0:CONV_WIDTH] = project(GLU_OFF)
    glu_ref[:, CONV_WIDTH:2 * CONV_WIDTH] = project(GLU_OFF + CONV_WIDTH)
    for r0 in range(0, tile, CONV_ROWS):
        rows = slice(r0, r0 + CONV_ROWS)
        conv_x[0, CONV_HALO + r0:CONV_HALO + r0 + CONV_ROWS, :] = (
            glu_ref[rows, 0:CONV_WIDTH]
            * jax.nn.sigmoid(glu_ref[rows, CONV_WIDTH:2 * CONV_WIDTH]))
        if r0 == CONV_ROWS:
            pool_ext[POOL_HALO:POOL_HALO + tile, :] = project(0)

    for s in range(1, SUBLANES):
        for c0 in range(0, CONV_WIDTH, LANES):
            cols = slice(c0, c0 + LANES)
            conv_x[s, :, cols] = pltpu.roll(conv_x[0, :, cols], conv_rows - s, axis=0)

    frame = t * tile + lax.broadcasted_iota(jnp.int32, (tile, 1), 0)
    for g, window in enumerate(POOL_WINDOWS):
        cols = slice(g * POOL_GROUP_DIM, (g + 1) * POOL_GROUP_DIM)
        ext = pool_ext[:, cols]
        s = ext
        step = 1
        while step < window:
            s = s + pltpu.roll(s, step, axis=0)
            step *= 2
        inv_count = 1.0 / jnp.minimum(frame + 1, window).astype(jnp.float32)
        pooled = s[POOL_HALO:, :] * inv_count - ext[POOL_HALO:, :]
        pooled_ref[:, cols] = pooled.astype(jnp.bfloat16)

    first_tap = CONV_HALO - (CONV_K - 1)
    for r0 in range(0, tile, CONV_ROWS):
        accs = []
        for c0 in range(0, CONV_WIDTH, LANES):
            cols = slice(c0, c0 + LANES)
            acc = jnp.broadcast_to(conv_b_ref[:, cols], (CONV_ROWS, LANES))
            for k in range(CONV_K):
                q, s = divmod(first_tap + k, SUBLANES)
                lo = r0 + q * SUBLANES
                acc = acc + conv_w_ref[k:k + 1, cols] * conv_x[s, lo:lo + CONV_ROWS, cols]
            accs.append(acc)
        conv_out[r0:r0 + CONV_ROWS, :] = jnp.concatenate(accs, axis=-1)

    def gate_piece(i, after):
        lhs = h_ref[...]
        if after is not None:
            lhs = lhs + _repeat(_zero_after(after).astype(jnp.bfloat16), tile, D_MODEL)
        c0 = i * PROJ_COLS
        piece = jax.nn.sigmoid(_dot(lhs, w_in_ref[:, GATE_A_OFF + c0:GATE_A_OFF + c0 + PROJ_COLS]))
        gate_ref[:, c0:c0 + PROJ_COLS] = piece
        return piece[tile - PACKED_ROWS:, PROJ_COLS - LANES:]

    def norm_stage(i, after):
        for r0 in range(i * stage_rows, (i + 1) * stage_rows, CONV_ROWS):
            c = conv_out[r0:r0 + CONV_ROWS, :]
            if after is not None:
                c = c + _repeat(_zero_after(after), CONV_ROWS, CONV_WIDTH)
            mu = jnp.mean(c, axis=-1, keepdims=True)
            cc = c - mu
            var = jnp.mean(cc * cc, axis=-1, keepdims=True)
            y = cc * lax.rsqrt(var + EPS) * ln_g_ref[...] + ln_b_ref[...]
            y = y * jax.nn.sigmoid(y)
            y_ref[r0:r0 + CONV_ROWS, :] = y.astype(jnp.bfloat16)
        return y[CONV_ROWS - PACKED_ROWS:, CONV_WIDTH - LANES:]

    n_pieces = 2 * D_MODEL // PROJ_COLS
    stage_rows = tile // n_pieces
    mxu_done = vpu_done = None
    for i in range(n_pieces):
        piece_done = gate_piece(i, vpu_done)
        vpu_done = norm_stage(i, mxu_done)
        mxu_done = piece_done

    for c0 in range(0, D_MODEL, PROJ_COLS):
        cols = slice(c0, c0 + PROJ_COLS)
        a = _dot(pooled_ref[...], w_pool_ref[:, cols])
        b = _dot(y_ref[...], w_conv_out_ref[:, cols])
        merged = (gate_ref[:, cols] * a
                  + gate_ref[:, D_MODEL + c0:D_MODEL + c0 + PROJ_COLS] * b)
        merged_ref[:, cols] = merged.astype(jnp.bfloat16)
    mix = _dot(merged_ref[...], w_o_ref[...])
    o_ref[...] = x_ref[...] + _rmsnorm(mix, g_post_ref[...])

    pool_ext[0:POOL_HALO, :] = pool_ext[tile:tile + POOL_HALO, :]
    conv_x[0, 0:CONV_HALO, :] = conv_x[0, tile:tile + CONV_HALO, :]


def _mlp_kernel(x_ref, g_pre_ref, w_up_hbm, w_down_hbm, g_post_ref, o_ref,
                w_up_ref, w_down_ref, stage_up, stage_down, sem):
    @pl.when(pl.program_id(0) == 0)
    def _():
        _load_as_bf16(w_up_hbm, w_up_ref, stage_up, sem, W_UP_CHUNK_ROWS)
        _load_as_bf16(w_down_hbm, w_down_ref, stage_down, sem, W_DOWN_CHUNK_ROWS)

    x = x_ref[...]
    h = _rmsnorm(x, g_pre_ref[...]).astype(jnp.bfloat16)
    acc = None
    for c0 in range(0, D_FF, FF_CHUNK):
        a = jnp.maximum(_dot(h, w_up_ref[:, c0:c0 + FF_CHUNK]), 0.0)
        p = _dot((a * a).astype(jnp.bfloat16), w_down_ref[c0:c0 + FF_CHUNK, :])
        acc = p if acc is None else acc + p
    o_ref[...] = x + _rmsnorm(acc, g_post_ref[...])


def _resident(shape):
    return pl.BlockSpec(shape, lambda *_: (0,) * len(shape), pipeline_mode=pl.Buffered(1))


def _mixer(x, g_pre, w_in, pool_w, pool_scale, w_pool_out, conv_w, layer, conv_b, ln_g, ln_b,
           w_conv_out, w_o, g_post):
    batch, seq, d = x.shape
    tile = MIXER_TILE
    assert seq % tile == 0 and d == D_MODEL
    x_spec = pl.BlockSpec((None, tile, d), lambda b, t: (b, t, 0))
    operands = (g_pre, w_in, pool_w, pool_scale, w_pool_out, conv_w, conv_b, ln_g, ln_b,
                w_conv_out, w_o, g_post)
    in_hbm = (w_in, w_pool_out, w_conv_out, w_o)
    hbm = pl.BlockSpec(memory_space=pl.ANY)
    conv_w_spec = pl.BlockSpec((None,) + conv_w.shape[1:], lambda *_: (layer, 0, 0),
                               pipeline_mode=pl.Buffered(1))

    def spec(a):
        if a is conv_w:
            return conv_w_spec
        return hbm if any(a is w for w in in_hbm) else _resident(a.shape)

    return pl.pallas_call(
        functools.partial(_mixer_kernel, tile=tile),
        grid=(batch, seq // tile),
        in_specs=[x_spec] + [spec(a) for a in operands],
        out_specs=x_spec,
        out_shape=jax.ShapeDtypeStruct(x.shape, x.dtype),
        scratch_shapes=[
            pltpu.VMEM(w_in.shape, jnp.bfloat16),
            pltpu.VMEM(w_pool_out.shape, jnp.bfloat16),
            pltpu.VMEM(w_conv_out.shape, jnp.bfloat16),
            pltpu.VMEM(w_o.shape, jnp.bfloat16),
            pltpu.VMEM((2, W_IN_CHUNK_ROWS, w_in.shape[1]), jnp.float32),
            pltpu.VMEM((2, W_OUT_CHUNK_ROWS, d), jnp.float32),
            pltpu.SemaphoreType.DMA((2,)),
            pltpu.VMEM((tile, d), jnp.bfloat16),
            pltpu.VMEM((POOL_HALO + tile, POOL_WIDTH), jnp.float32),
            pltpu.VMEM((tile, POOL_WIDTH), jnp.bfloat16),
            pltpu.VMEM((tile, 2 * CONV_WIDTH), jnp.float32),
            pltpu.VMEM((SUBLANES, CONV_HALO + tile, CONV_WIDTH), jnp.float32),
            pltpu.VMEM((tile, CONV_WIDTH), jnp.float32),
            pltpu.VMEM((tile, CONV_WIDTH), jnp.bfloat16),
            pltpu.VMEM((tile, 2 * d), jnp.float32),
            pltpu.VMEM((tile, d), jnp.bfloat16),
        ],
        compiler_params=pltpu.CompilerParams(
            dimension_semantics=("arbitrary", "arbitrary"),
            vmem_limit_bytes=VMEM_LIMIT_BYTES),
        name="mixer",
    )(x, *operands)


def _mlp(x, g_pre, w_up, w_down, g_post):
    tokens, d = x.shape
    tile = MLP_TILE
    assert tokens % tile == 0 and d == D_MODEL
    x_spec = pl.BlockSpec((tile, d), lambda i: (i, 0))
    operands = (g_pre, w_up, w_down, g_post)
    return pl.pallas_call(
        _mlp_kernel,
        grid=(tokens // tile,),
        in_specs=[x_spec, _resident(g_pre.shape), pl.BlockSpec(memory_space=pl.ANY),
                  pl.BlockSpec(memory_space=pl.ANY), _resident(g_post.shape)],
        out_specs=x_spec,
        out_shape=jax.ShapeDtypeStruct(x.shape, x.dtype),
        scratch_shapes=[
            pltpu.VMEM(w_up.shape, jnp.bfloat16),
            pltpu.VMEM(w_down.shape, jnp.bfloat16),
            pltpu.VMEM((2, W_UP_CHUNK_ROWS, w_up.shape[1]), jnp.float32),
            pltpu.VMEM((2, W_DOWN_CHUNK_ROWS, w_down.shape[1]), jnp.float32),
            pltpu.SemaphoreType.DMA((2,)),
        ],
        compiler_params=pltpu.CompilerParams(
            dimension_semantics=("arbitrary",),
            vmem_limit_bytes=VMEM_LIMIT_BYTES),
        name="mlp",
    )(x, *operands)


def kernel(x, norm_mix_pre, w_in, pool_w, pool_scale, w_pool_out, conv_w, conv_b, conv_ln_g,
           conv_ln_b, w_conv_out, w_o, norm_mix_post, norm_mlp_pre, w_up, w_down, norm_mlp_post):
    batch, seq, d = x.shape
    row = lambda p: p.reshape(1, -1)
    for l in range(w_in.shape[0]):
        x = _mixer(x, row(norm_mix_pre[l]), w_in[l], pool_w[l],
                   pool_scale[l].reshape(len(POOL_WINDOWS), 1, POOL_GROUP_DIM), w_pool_out[l],
                   conv_w, l, row(conv_b[l]), row(conv_ln_g[l]), row(conv_ln_b[l]), w_conv_out[l],
                   w_o[l], row(norm_mix_post[l]))
        x = _mlp(x.reshape(batch * seq, d), row(norm_mlp_pre[l]), w_up[l], w_down[l],
                 row(norm_mlp_post[l])).reshape(batch, seq, d)
    return x
```

```python
import functools

import jax
import jax.numpy as jnp
from jax import lax
from jax.experimental import pallas as pl
from jax.experimental.pallas import tpu as pltpu

D_MODEL = 1024
POOL_WIDTH = 512
POOL_GROUP_DIM = 128
POOL_WINDOWS = (2, 4, 8, 16)
CONV_WIDTH = 512
CONV_K = 31
D_FF = 4 * D_MODEL
EPS = 1e-6

SUBLANES = 8
LANES = 128
POOL_HALO = 16
CONV_HALO = 32
GLU_OFF = POOL_WIDTH
GATE_A_OFF = POOL_WIDTH + 2 * CONV_WIDTH
_GLU_HALF = CONV_WIDTH // 2
W_IN_COL_STARTS = (
    list(range(0, GLU_OFF, LANES))
    + [GLU_OFF + part * CONV_WIDTH + j * _GLU_HALF + c
       for j in range(2) for part in range(2) for c in range(0, _GLU_HALF, LANES)]
    + list(range(GATE_A_OFF, GATE_A_OFF + 2 * D_MODEL, LANES)))

MIXER_TILE = 512
MLP_TILE = 1024
FF_CHUNK = 1024
PROJ_COLS = 512
CONV_ROWS = 64
PACKED_ROWS = 16
W_IN_CHUNK_ROWS = 128
W_OUT_CHUNK_ROWS = 256
W_UP_CHUNK_ROWS = 128
W_DOWN_CHUNK_ROWS = 512
VMEM_LIMIT_BYTES = 56 * 1024 * 1024


def _dot(a, b):
    return jnp.dot(a, b, preferred_element_type=jnp.float32)


def _rmsnorm(x, g):
    ms = jnp.mean(x * x, axis=-1, keepdims=True)
    return x * lax.rsqrt(ms + EPS) * g


def _zero_after(v):
    return jnp.minimum(jnp.abs(v), 0.0)


def _repeat(v, rows, cols):
    v = jnp.concatenate([v] * (cols // v.shape[1]), axis=1)
    return jnp.concatenate([v] * (rows // v.shape[0]), axis=0)


def _stream_rows(src_hbm, stage, sem, chunk_rows, consume):
    n_chunks = src_hbm.shape[0] // chunk_rows

    def copy(c):
        return pltpu.make_async_copy(src_hbm.at[pl.ds(c * chunk_rows, chunk_rows), :],
                                     stage.at[c % 2], sem.at[c % 2])

    copy(0).start()
    for c in range(n_chunks):
        if c + 1 < n_chunks:
            copy(c + 1).start()
        copy(c).wait()
        consume(c, stage[c % 2])


def _load_as_bf16(src_hbm, dst_ref, stage, sem, chunk_rows, col_starts=None):
    def cast(c, chunk):
        if col_starts is not None:
            chunk = jnp.concatenate([chunk[:, c0:c0 + LANES] for c0 in col_starts], axis=1)
        dst_ref[c * chunk_rows:(c + 1) * chunk_rows, :] = chunk.astype(jnp.bfloat16)

    _stream_rows(src_hbm, stage, sem, chunk_rows, cast)


def _load_folded_pool(w_pool_out_hbm, pool_w_ref, pool_scale_ref, dst_ref, stage, sem, chunk_rows):
    groups_per_chunk = chunk_rows // POOL_GROUP_DIM

    def fold(c, chunk):
        for j in range(groups_per_chunk):
            g = c * groups_per_chunk + j
            scaled = pool_w_ref[g] * pool_scale_ref[g]
            rows = chunk[j * POOL_GROUP_DIM:(j + 1) * POOL_GROUP_DIM, :]
            folded = jnp.dot(scaled, rows, precision=lax.Precision.HIGHEST,
                             preferred_element_type=jnp.float32)
            dst_ref[g * POOL_GROUP_DIM:(g + 1) * POOL_GROUP_DIM, :] = folded.astype(jnp.bfloat16)

    _stream_rows(w_pool_out_hbm, stage, sem, chunk_rows, fold)


def _mixer_kernel(x_ref, g_pre_ref, w_in_hbm, pool_w_ref, pool_scale_ref, w_pool_out_hbm,
                  conv_w_ref, conv_b_ref, ln_g_ref, ln_b_ref, w_conv_out_hbm, w_o_hbm,
                  g_post_ref, o_ref,
                  w_in_ref, w_pool_ref, w_conv_out_ref, w_o_ref, stage_in, stage_out, sem,
                  h_ref, pool_ext, pooled_ref, glu_ref, conv_x, conv_out, y_ref, gate_ref, merged_ref,
                  *, tile):
    t = pl.program_id(1)
    conv_rows = CONV_HALO + tile

    @pl.when((pl.program_id(0) == 0) & (t == 0))
    def _():
        _load_as_bf16(w_in_hbm, w_in_ref, stage_in, sem, W_IN_CHUNK_ROWS, col_starts=W_IN_COL_STARTS)
        _load_folded_pool(w_pool_out_hbm, pool_w_ref, pool_scale_ref, w_pool_ref, stage_out, sem,
                          W_OUT_CHUNK_ROWS)
        _load_as_bf16(w_conv_out_hbm, w_conv_out_ref, stage_out, sem, W_OUT_CHUNK_ROWS)
        _load_as_bf16(w_o_hbm, w_o_ref, stage_out, sem, W_OUT_CHUNK_ROWS)

    @pl.when(t == 0)
    def _():
        pool_ext[0:POOL_HALO, :] = jnp.zeros((POOL_HALO, POOL_WIDTH), jnp.float32)
        conv_x[0, 0:CONV_HALO, :] = jnp.zeros((CONV_HALO, CONV_WIDTH), jnp.float32)

    def project(c0):
        return _dot(h_ref[...], w_in_ref[:, c0:c0 + PROJ_COLS])

    h_ref[...] = _rmsnorm(x_ref[...], g_pre_ref[...]).astype(jnp.bfloat16)

    half = CONV_WIDTH // 2
    for j in range(2):
        glu_ref[:, j * PROJ_COLS:(j + 1) * PROJ_COLS] = project(GLU_OFF + j * PROJ_COLS)
        for r0 in range(0, tile, CONV_ROWS):
            rows = slice(r0, r0 + CONV_ROWS)
            conv_x[0, CONV_HALO + r0:CONV_HALO + r0 + CONV_ROWS, j * half:(j + 1) * half] = (
                glu_ref[rows, j * PROJ_COLS:j * PROJ_COLS + half]
                * jax.nn.sigmoid(glu_ref[rows, j * PROJ_COLS + half:(j + 1) * PROJ_COLS]))
        if j == 0:
            pool_ext[POOL_HALO:POOL_HALO + tile, :] = project(0)
        for c0 in range(j * half, (j + 1) * half, LANES):
            cols = slice(c0, c0 + LANES)
            x0 = conv_x[0, :, cols]
            for s in range(1, SUBLANES):
                conv_x[s, :, cols] = pltpu.roll(x0, conv_rows - s, axis=0)

    frame = t * tile + lax.broadcasted_iota(jnp.int32, (tile, 1), 0)
    for g, window in enumerate(POOL_WINDOWS):
        cols = slice(g * POOL_GROUP_DIM, (g + 1) * POOL_GROUP_DIM)
        ext = pool_ext[:, cols]
        s = ext
        step = 1
        while step < window:
            s = s + pltpu.roll(s, step, axis=0)
            step *= 2
        inv_count = 1.0 / jnp.minimum(frame + 1, window).astype(jnp.float32)
        pooled = s[POOL_HALO:, :] * inv_count - ext[POOL_HALO:, :]
        pooled_ref[:, cols] = pooled.astype(jnp.bfloat16)

    first_tap = CONV_HALO - (CONV_K - 1)
    for r0 in range(0, tile, CONV_ROWS):
        accs = []
        for c0 in range(0, CONV_WIDTH, LANES):
            cols = slice(c0, c0 + LANES)
            acc = jnp.broadcast_to(conv_b_ref[:, cols], (CONV_ROWS, LANES))
            for k in range(CONV_K):
                q, s = divmod(first_tap + k, SUBLANES)
                lo = r0 + q * SUBLANES
                acc = acc + conv_w_ref[k:k + 1, cols] * conv_x[s, lo:lo + CONV_ROWS, cols]
            accs.append(acc)
        conv_out[r0:r0 + CONV_ROWS, :] = jnp.concatenate(accs, axis=-1)

    def gate_piece(i, after):
        lhs = h_ref[...]
        if after is not None:
            lhs = lhs + _repeat(_zero_after(after).astype(jnp.bfloat16), tile, D_MODEL)
        c0 = i * PROJ_COLS
        piece = jax.nn.sigmoid(_dot(lhs, w_in_ref[:, GATE_A_OFF + c0:GATE_A_OFF + c0 + PROJ_COLS]))
        gate_ref[:, c0:c0 + PROJ_COLS] = piece
        return piece[tile - PACKED_ROWS:, PROJ_COLS - LANES:]

    def norm_stage(i, after):
        for r0 in range(i * stage_rows, (i + 1) * stage_rows, CONV_ROWS):
            c = conv_out[r0:r0 + CONV_ROWS, :]
            if after is not None:
                c = c + _repeat(_zero_after(after), CONV_ROWS, CONV_WIDTH)
            mu = jnp.mean(c, axis=-1, keepdims=True)
            cc = c - mu
            var = jnp.mean(cc * cc, axis=-1, keepdims=True)
            y = cc * lax.rsqrt(var + EPS) * ln_g_ref[...] + ln_b_ref[...]
            y = y * jax.nn.sigmoid(y)
            y_ref[r0:r0 + CONV_ROWS, :] = y.astype(jnp.bfloat16)
        return y[CONV_ROWS - PACKED_ROWS:, CONV_WIDTH - LANES:]

    n_pieces = 2 * D_MODEL // PROJ_COLS
    stage_rows = tile // n_pieces
    mxu_done = vpu_done = None
    for i in range(n_pieces):
        piece_done = gate_piece(i, vpu_done)
        vpu_done = norm_stage(i, mxu_done)
        mxu_done = piece_done

    for c0 in range(0, D_MODEL, PROJ_COLS):
        cols = slice(c0, c0 + PROJ_COLS)
        a = _dot(pooled_ref[...], w_pool_ref[:, cols])
        b = _dot(y_ref[...], w_conv_out_ref[:, cols])
        merged = (gate_ref[:, cols] * a
                  + gate_ref[:, D_MODEL + c0:D_MODEL + c0 + PROJ_COLS] * b)
        merged_ref[:, cols] = merged.astype(jnp.bfloat16)
    mix = _dot(merged_ref[...], w_o_ref[...])
    o_ref[...] = x_ref[...] + _rmsnorm(mix, g_post_ref[...])

    pool_ext[0:POOL_HALO, :] = pool_ext[tile:tile + POOL_HALO, :]
    conv_x[0, 0:CONV_HALO, :] = conv_x[0, tile:tile + CONV_HALO, :]


def _mlp_kernel(x_ref, g_pre_ref, w_up_hbm, w_down_hbm, g_post_ref, o_ref,
                w_up_ref, w_down_ref, stage_up, stage_down, sem):
    @pl.when(pl.program_id(0) == 0)
    def _():
        _load_as_bf16(w_up_hbm, w_up_ref, stage_up, sem, W_UP_CHUNK_ROWS)
        _load_as_bf16(w_down_hbm, w_down_ref, stage_down, sem, W_DOWN_CHUNK_ROWS)

    x = x_ref[...]
    h = _rmsnorm(x, g_pre_ref[...]).astype(jnp.bfloat16)
    acc = None
    for c0 in range(0, D_FF, FF_CHUNK):
        a = jnp.maximum(_dot(h, w_up_ref[:, c0:c0 + FF_CHUNK]), 0.0)
        p = _dot((a * a).astype(jnp.bfloat16), w_down_ref[c0:c0 + FF_CHUNK, :])
        acc = p if acc is None else acc + p
    o_ref[...] = x + _rmsnorm(acc, g_post_ref[...])


def _resident(shape):
    return pl.BlockSpec(shape, lambda *_: (0,) * len(shape), pipeline_mode=pl.Buffered(1))


def _mixer(x, g_pre, w_in, pool_w, pool_scale, w_pool_out, conv_w, layer, conv_b, ln_g, ln_b,
           w_conv_out, w_o, g_post):
    batch, seq, d = x.shape
    tile = MIXER_TILE
    assert seq % tile == 0 and d == D_MODEL
    x_spec = pl.BlockSpec((None, tile, d), lambda b, t: (b, t, 0))
    operands = (g_pre, w_in, pool_w, pool_scale, w_pool_out, conv_w, conv_b, ln_g, ln_b,
                w_conv_out, w_o, g_post)
    in_hbm = (w_in, w_pool_out, w_conv_out, w_o)
    hbm = pl.BlockSpec(memory_space=pl.ANY)
    conv_w_spec = pl.BlockSpec((None,) + conv_w.shape[1:], lambda *_: (layer, 0, 0),
                               pipeline_mode=pl.Buffered(1))

    def spec(a):
        if a is conv_w:
            return conv_w_spec
        return hbm if any(a is w for w in in_hbm) else _resident(a.shape)

    return pl.pallas_call(
        functools.partial(_mixer_kernel, tile=tile),
        grid=(batch, seq // tile),
        in_specs=[x_spec] + [spec(a) for a in operands],
        out_specs=x_spec,
        out_shape=jax.ShapeDtypeStruct(x.shape, x.dtype),
        scratch_shapes=[
            pltpu.VMEM(w_in.shape, jnp.bfloat16),
            pltpu.VMEM(w_pool_out.shape, jnp.bfloat16),
            pltpu.VMEM(w_conv_out.shape, jnp.bfloat16),
            pltpu.VMEM(w_o.shape, jnp.bfloat16),
            pltpu.VMEM((2, W_IN_CHUNK_ROWS, w_in.shape[1]), jnp.float32),
            pltpu.VMEM((2, W_OUT_CHUNK_ROWS, d), jnp.float32),
            pltpu.SemaphoreType.DMA((2,)),
            pltpu.VMEM((tile, d), jnp.bfloat16),
            pltpu.VMEM((POOL_HALO + tile, POOL_WIDTH), jnp.float32),
            pltpu.VMEM((tile, POOL_WIDTH), jnp.bfloat16),
            pltpu.VMEM((tile, 2 * CONV_WIDTH), jnp.float32),
            pltpu.VMEM((SUBLANES, CONV_HALO + tile, CONV_WIDTH), jnp.float32),
            pltpu.VMEM((tile, CONV_WIDTH), jnp.float32),
            pltpu.VMEM((tile, CONV_WIDTH), jnp.bfloat16),
            pltpu.VMEM((tile, 2 * d), jnp.float32),
            pltpu.VMEM((tile, d), jnp.bfloat16),
        ],
        compiler_params=pltpu.CompilerParams(
            dimension_semantics=("arbitrary", "arbitrary"),
            vmem_limit_bytes=VMEM_LIMIT_BYTES),
        name="mixer",
    )(x, *operands)


def _mlp(x, g_pre, w_up, w_down, g_post):
    tokens, d = x.shape
    tile = MLP_TILE
    assert tokens % tile == 0 and d == D_MODEL
    x_spec = pl.BlockSpec((tile, d), lambda i: (i, 0))
    operands = (g_pre, w_up, w_down, g_post)
    return pl.pallas_call(
        _mlp_kernel,
        grid=(tokens // tile,),
        in_specs=[x_spec, _resident(g_pre.shape), pl.BlockSpec(memory_space=pl.ANY),
                  pl.BlockSpec(memory_space=pl.ANY), _resident(g_post.shape)],
        out_specs=x_spec,
        out_shape=jax.ShapeDtypeStruct(x.shape, x.dtype),
        scratch_shapes=[
            pltpu.VMEM(w_up.shape, jnp.bfloat16),
            pltpu.VMEM(w_down.shape, jnp.bfloat16),
            pltpu.VMEM((2, W_UP_CHUNK_ROWS, w_up.shape[1]), jnp.float32),
            pltpu.VMEM((2, W_DOWN_CHUNK_ROWS, w_down.shape[1]), jnp.float32),
            pltpu.SemaphoreType.DMA((2,)),
        ],
        compiler_params=pltpu.CompilerParams(
            dimension_semantics=("arbitrary",),
            vmem_limit_bytes=VMEM_LIMIT_BYTES),
        name="mlp",
    )(x, *operands)


def kernel(x, norm_mix_pre, w_in, pool_w, pool_scale, w_pool_out, conv_w, conv_b, conv_ln_g,
           conv_ln_b, w_conv_out, w_o, norm_mix_post, norm_mlp_pre, w_up, w_down, norm_mlp_post):
    batch, seq, d = x.shape
    row = lambda p: p.reshape(1, -1)
    for l in range(w_in.shape[0]):
        x = _mixer(x, row(norm_mix_pre[l]), w_in[l], pool_w[l],
                   pool_scale[l].reshape(len(POOL_WINDOWS), 1, POOL_GROUP_DIM), w_pool_out[l],
                   conv_w, l, row(conv_b[l]), row(conv_ln_g[l]), row(conv_ln_b[l]), w_conv_out[l],
                   w_o[l], row(norm_mix_post[l]))
        x = _mlp(x.reshape(batch * seq, d), row(norm_mlp_pre[l]), w_up[l], w_down[l],
                 row(norm_mlp_post[l])).reshape(batch, seq, d)
    return x
```

```python
import functools

import jax
import jax.numpy as jnp
from jax import lax
from jax.experimental import pallas as pl
from jax.experimental.pallas import tpu as pltpu

D_MODEL = 1024
POOL_WIDTH = 512
POOL_GROUP_DIM = 128
POOL_WINDOWS = (2, 4, 8, 16)
CONV_WIDTH = 512
CONV_K = 31
D_FF = 4 * D_MODEL
EPS = 1e-6

SUBLANES = 8
LANES = 128
POOL_HALO = 16
CONV_HALO = 32
CONV_PITCH = CONV_WIDTH + LANES
GLU_OFF = POOL_WIDTH
GATE_A_OFF = POOL_WIDTH + 2 * CONV_WIDTH
_GLU_HALF = CONV_WIDTH // 2
W_IN_COL_STARTS = (
    list(range(0, GLU_OFF, LANES))
    + [GLU_OFF + part * CONV_WIDTH + j * _GLU_HALF + c
       for j in range(2) for part in range(2) for c in range(0, _GLU_HALF, LANES)]
    + list(range(GATE_A_OFF, GATE_A_OFF + 2 * D_MODEL, LANES)))

MIXER_TILE = 512
MLP_TILE = 1024
FF_CHUNK = 1024
PROJ_COLS = 512
CONV_ROWS = 64
PACKED_ROWS = 16
W_IN_CHUNK_ROWS = 128
W_OUT_CHUNK_ROWS = 256
W_UP_CHUNK_ROWS = 128
W_DOWN_CHUNK_ROWS = 512
VMEM_LIMIT_BYTES = 56 * 1024 * 1024


def _dot(a, b):
    return jnp.dot(a, b, preferred_element_type=jnp.float32)


def _rmsnorm(x, g):
    ms = jnp.mean(x * x, axis=-1, keepdims=True)
    return x * lax.rsqrt(ms + EPS) * g


def _zero_after(v):
    return jnp.minimum(jnp.abs(v), 0.0)


def _repeat(v, rows, cols):
    v = jnp.concatenate([v] * (cols // v.shape[1]), axis=1)
    return jnp.concatenate([v] * (rows // v.shape[0]), axis=0)


def _stream_rows(src_hbm, stage, sem, chunk_rows, consume):
    n_chunks = src_hbm.shape[0] // chunk_rows

    def copy(c):
        return pltpu.make_async_copy(src_hbm.at[pl.ds(c * chunk_rows, chunk_rows), :],
                                     stage.at[c % 2], sem.at[c % 2])

    copy(0).start()
    for c in range(n_chunks):
        if c + 1 < n_chunks:
            copy(c + 1).start()
        copy(c).wait()
        consume(c, stage[c % 2])


def _load_as_bf16(src_hbm, dst_ref, stage, sem, chunk_rows, col_starts=None):
    def cast(c, chunk):
        if col_starts is not None:
            chunk = jnp.concatenate([chunk[:, c0:c0 + LANES] for c0 in col_starts], axis=1)
        dst_ref[c * chunk_rows:(c + 1) * chunk_rows, :] = chunk.astype(jnp.bfloat16)

    _stream_rows(src_hbm, stage, sem, chunk_rows, cast)


def _load_folded_pool(w_pool_out_hbm, pool_w_ref, pool_scale_ref, dst_ref, stage, sem, chunk_rows):
    groups_per_chunk = chunk_rows // POOL_GROUP_DIM

    def fold(c, chunk):
        for j in range(groups_per_chunk):
            g = c * groups_per_chunk + j
            scaled = pool_w_ref[g] * pool_scale_ref[g]
            rows = chunk[j * POOL_GROUP_DIM:(j + 1) * POOL_GROUP_DIM, :]
            folded = jnp.dot(scaled, rows, precision=lax.Precision.HIGHEST,
                             preferred_element_type=jnp.float32)
            dst_ref[g * POOL_GROUP_DIM:(g + 1) * POOL_GROUP_DIM, :] = folded.astype(jnp.bfloat16)

    _stream_rows(w_pool_out_hbm, stage, sem, chunk_rows, fold)


def _mixer_kernel(x_ref, g_pre_ref, w_in_hbm, pool_w_ref, pool_scale_ref, w_pool_out_hbm,
                  conv_w_ref, conv_b_ref, ln_g_ref, ln_b_ref, w_conv_out_hbm, w_o_hbm,
                  g_post_ref, o_ref,
                  w_in_ref, w_pool_ref, w_conv_out_ref, w_o_ref, stage_in, stage_out, sem,
                  h_ref, pool_ext, pooled_ref, glu_ref, conv_x, conv_out, y_ref, gate_ref, merged_ref,
                  *, tile):
    t = pl.program_id(1)
    conv_rows = CONV_HALO + tile

    @pl.when((pl.program_id(0) == 0) & (t == 0))
    def _():
        _load_as_bf16(w_in_hbm, w_in_ref, stage_in, sem, W_IN_CHUNK_ROWS, col_starts=W_IN_COL_STARTS)
        _load_folded_pool(w_pool_out_hbm, pool_w_ref, pool_scale_ref, w_pool_ref, stage_out, sem,
                          W_OUT_CHUNK_ROWS)
        _load_as_bf16(w_conv_out_hbm, w_conv_out_ref, stage_out, sem, W_OUT_CHUNK_ROWS)
        _load_as_bf16(w_o_hbm, w_o_ref, stage_out, sem, W_OUT_CHUNK_ROWS)

    @pl.when(t == 0)
    def _():
        pool_ext[0:POOL_HALO, :] = jnp.zeros((POOL_HALO, POOL_WIDTH), jnp.float32)
        conv_x[0, 0:CONV_HALO, 0:CONV_WIDTH] = jnp.zeros((CONV_HALO, CONV_WIDTH), jnp.float32)

    def project(c0):
        return _dot(h_ref[...], w_in_ref[:, c0:c0 + PROJ_COLS])

    h_ref[...] = _rmsnorm(x_ref[...], g_pre_ref[...]).astype(jnp.bfloat16)

    half = CONV_WIDTH // 2
    for j in range(2):
        glu_ref[:, j * PROJ_COLS:(j + 1) * PROJ_COLS] = project(GLU_OFF + j * PROJ_COLS)
        for r0 in range(0, tile, CONV_ROWS):
            rows = slice(r0, r0 + CONV_ROWS)
            conv_x[0, CONV_HALO + r0:CONV_HALO + r0 + CONV_ROWS, j * half:(j + 1) * half] = (
                glu_ref[rows, j * PROJ_COLS:j * PROJ_COLS + half]
                * jax.nn.sigmoid(glu_ref[rows, j * PROJ_COLS + half:(j + 1) * PROJ_COLS]))
        if j == 0:
            pool_ext[POOL_HALO:POOL_HALO + tile, :] = project(0)
        for c0 in range(j * half, (j + 1) * half, LANES):
            cols = slice(c0, c0 + LANES)
            x0 = conv_x[0, :, cols]
            for s in range(1, SUBLANES):
                conv_x[s, :, cols] = pltpu.roll(x0, conv_rows - s, axis=0)

    frame = t * tile + lax.broadcasted_iota(jnp.int32, (tile, 1), 0)
    for g, window in enumerate(POOL_WINDOWS):
        cols = slice(g * POOL_GROUP_DIM, (g + 1) * POOL_GROUP_DIM)
        ext = pool_ext[:, cols]
        s = ext
        step = 1
        while step < window:
            s = s + pltpu.roll(s, step, axis=0)
            step *= 2
        inv_count = 1.0 / jnp.minimum(frame + 1, window).astype(jnp.float32)
        pooled = s[POOL_HALO:, :] * inv_count - ext[POOL_HALO:, :]
        pooled_ref[:, cols] = pooled.astype(jnp.bfloat16)

    first_tap = CONV_HALO - (CONV_K - 1)
    for r0 in range(0, tile, CONV_ROWS):
        accs = []
        for c0 in range(0, CONV_WIDTH, LANES):
            cols = slice(c0, c0 + LANES)
            acc = jnp.broadcast_to(conv_b_ref[:, cols], (CONV_ROWS, LANES))
            for k in range(CONV_K):
                q, s = divmod(first_tap + k, SUBLANES)
                lo = r0 + q * SUBLANES
                acc = acc + conv_w_ref[k:k + 1, cols] * conv_x[s, lo:lo + CONV_ROWS, cols]
            accs.append(acc)
        conv_out[r0:r0 + CONV_ROWS, :] = jnp.concatenate(accs, axis=-1)

    def gate_piece(i, after):
        lhs = h_ref[...]
        if after is not None:
            lhs = lhs + _repeat(_zero_after(after).astype(jnp.bfloat16), tile, D_MODEL)
        c0 = i * PROJ_COLS
        piece = jax.nn.sigmoid(_dot(lhs, w_in_ref[:, GATE_A_OFF + c0:GATE_A_OFF + c0 + PROJ_COLS]))
        gate_ref[:, c0:c0 + PROJ_COLS] = piece
        return piece[tile - PACKED_ROWS:, PROJ_COLS - LANES:]

    def norm_stage(i, after):
        for r0 in range(i * stage_rows, (i + 1) * stage_rows, CONV_ROWS):
            c = conv_out[r0:r0 + CONV_ROWS, :]
            if after is not None:
                c = c + _repeat(_zero_after(after), CONV_ROWS, CONV_WIDTH)
            mu = jnp.mean(c, axis=-1, keepdims=True)
            cc = c - mu
            var = jnp.mean(cc * cc, axis=-1, keepdims=True)
            y = cc * lax.rsqrt(var + EPS) * ln_g_ref[...] + ln_b_ref[...]
            y = y * jax.nn.sigmoid(y)
            y_ref[r0:r0 + CONV_ROWS, :] = y.astype(jnp.bfloat16)
        return y[CONV_ROWS - PACKED_ROWS:, CONV_WIDTH - LANES:]

    n_pieces = 2 * D_MODEL // PROJ_COLS
    stage_rows = tile // n_pieces
    mxu_done = vpu_done = None
    for i in range(n_pieces):
        piece_done = gate_piece(i, vpu_done)
        vpu_done = norm_stage(i, mxu_done)
        mxu_done = piece_done

    for c0 in range(0, D_MODEL, PROJ_COLS):
        cols = slice(c0, c0 + PROJ_COLS)
        a = _dot(pooled_ref[...], w_pool_ref[:, cols])
        b = _dot(y_ref[...], w_conv_out_ref[:, cols])
        merged = (gate_ref[:, cols] * a
                  + gate_ref[:, D_MODEL + c0:D_MODEL + c0 + PROJ_COLS] * b)
        merged_ref[:, cols] = merged.astype(jnp.bfloat16)
    mix = _dot(merged_ref[...], w_o_ref[...])
    o_ref[...] = x_ref[...] + _rmsnorm(mix, g_post_ref[...])

    pool_ext[0:POOL_HALO, :] = pool_ext[tile:tile + POOL_HALO, :]
    conv_x[0, 0:CONV_HALO, 0:CONV_WIDTH] = conv_x[0, tile:tile + CONV_HALO, 0:CONV_WIDTH]


def _mlp_kernel(x_ref, g_pre_ref, w_up_hbm, w_down_hbm, g_post_ref, o_ref,
                w_up_ref, w_down_ref, stage_up, stage_down, sem):
    @pl.when(pl.program_id(0) == 0)
    def _():
        _load_as_bf16(w_up_hbm, w_up_ref, stage_up, sem, W_UP_CHUNK_ROWS)
        _load_as_bf16(w_down_hbm, w_down_ref, stage_down, sem, W_DOWN_CHUNK_ROWS)

    x = x_ref[...]
    h = _rmsnorm(x, g_pre_ref[...]).astype(jnp.bfloat16)
    acc = None
    for c0 in range(0, D_FF, FF_CHUNK):
        a = jnp.maximum(_dot(h, w_up_ref[:, c0:c0 + FF_CHUNK]), 0.0)
        p = _dot((a * a).astype(jnp.bfloat16), w_down_ref[c0:c0 + FF_CHUNK, :])
        acc = p if acc is None else acc + p
    o_ref[...] = x + _rmsnorm(acc, g_post_ref[...])


def _resident(shape):
    return pl.BlockSpec(shape, lambda *_: (0,) * len(shape), pipeline_mode=pl.Buffered(1))


def _mixer(x, g_pre, w_in, pool_w, pool_scale, w_pool_out, conv_w, layer, conv_b, ln_g, ln_b,
           w_conv_out, w_o, g_post):
    batch, seq, d = x.shape
    tile = MIXER_TILE
    assert seq % tile == 0 and d == D_MODEL
    x_spec = pl.BlockSpec((None, tile, d), lambda b, t: (b, t, 0))
    operands = (g_pre, w_in, pool_w, pool_scale, w_pool_out, conv_w, conv_b, ln_g, ln_b,
                w_conv_out, w_o, g_post)
    in_hbm = (w_in, w_pool_out, w_conv_out, w_o)
    hbm = pl.BlockSpec(memory_space=pl.ANY)
    conv_w_spec = pl.BlockSpec((None,) + conv_w.shape[1:], lambda *_: (layer, 0, 0),
                               pipeline_mode=pl.Buffered(1))

    def spec(a):
        if a is conv_w:
            return conv_w_spec
        return hbm if any(a is w for w in in_hbm) else _resident(a.shape)

    return pl.pallas_call(
        functools.partial(_mixer_kernel, tile=tile),
        grid=(batch, seq // tile),
        in_specs=[x_spec] + [spec(a) for a in operands],
        out_specs=x_spec,
        out_shape=jax.ShapeDtypeStruct(x.shape, x.dtype),
        scratch_shapes=[
            pltpu.VMEM(w_in.shape, jnp.bfloat16),
            pltpu.VMEM(w_pool_out.shape, jnp.bfloat16),
            pltpu.VMEM(w_conv_out.shape, jnp.bfloat16),
            pltpu.VMEM(w_o.shape, jnp.bfloat16),
            pltpu.VMEM((2, W_IN_CHUNK_ROWS, w_in.shape[1]), jnp.float32),
            pltpu.VMEM((2, W_OUT_CHUNK_ROWS, d), jnp.float32),
            pltpu.SemaphoreType.DMA((2,)),
            pltpu.VMEM((tile, d), jnp.bfloat16),
            pltpu.VMEM((POOL_HALO + tile, POOL_WIDTH), jnp.float32),
            pltpu.VMEM((tile, POOL_WIDTH), jnp.bfloat16),
            pltpu.VMEM((tile, 2 * CONV_WIDTH), jnp.float32),
            pltpu.VMEM((SUBLANES, CONV_HALO + tile, CONV_PITCH), jnp.float32),
            pltpu.VMEM((tile, CONV_WIDTH), jnp.float32),
            pltpu.VMEM((tile, CONV_WIDTH), jnp.bfloat16),
            pltpu.VMEM((tile, 2 * d), jnp.float32),
            pltpu.VMEM((tile, d), jnp.bfloat16),
        ],
        compiler_params=pltpu.CompilerParams(
            dimension_semantics=("arbitrary", "arbitrary"),
            vmem_limit_bytes=VMEM_LIMIT_BYTES),
        name="mixer",
    )(x, *operands)


def _mlp(x, g_pre, w_up, w_down, g_post):
    tokens, d = x.shape
    tile = MLP_TILE
    assert tokens % tile == 0 and d == D_MODEL
    x_spec = pl.BlockSpec((tile, d), lambda i: (i, 0))
    operands = (g_pre, w_up, w_down, g_post)
    return pl.pallas_call(
        _mlp_kernel,
        grid=(tokens // tile,),
        in_specs=[x_spec, _resident(g_pre.shape), pl.BlockSpec(memory_space=pl.ANY),
                  pl.BlockSpec(memory_space=pl.ANY), _resident(g_post.shape)],
        out_specs=x_spec,
        out_shape=jax.ShapeDtypeStruct(x.shape, x.dtype),
        scratch_shapes=[
            pltpu.VMEM(w_up.shape, jnp.bfloat16),
            pltpu.VMEM(w_down.shape, jnp.bfloat16),
            pltpu.VMEM((2, W_UP_CHUNK_ROWS, w_up.shape[1]), jnp.float32),
            pltpu.VMEM((2, W_DOWN_CHUNK_ROWS, w_down.shape[1]), jnp.float32),
            pltpu.SemaphoreType.DMA((2,)),
        ],
        compiler_params=pltpu.CompilerParams(
            dimension_semantics=("arbitrary",),
            vmem_limit_bytes=VMEM_LIMIT_BYTES),
        name="mlp",
    )(x, *operands)


def kernel(x, norm_mix_pre, w_in, pool_w, pool_scale, w_pool_out, conv_w, conv_b, conv_ln_g,
           conv_ln_b, w_conv_out, w_o, norm_mix_post, norm_mlp_pre, w_up, w_down, norm_mlp_post):
    batch, seq, d = x.shape
    row = lambda p: p.reshape(1, -1)
    for l in range(w_in.shape[0]):
        x = _mixer(x, row(norm_mix_pre[l]), w_in[l], pool_w[l],
                   pool_scale[l].reshape(len(POOL_WINDOWS), 1, POOL_GROUP_DIM), w_pool_out[l],
                   conv_w, l, row(conv_b[l]), row(conv_ln_g[l]), row(conv_ln_b[l]), w_conv_out[l],
                   w_o[l], row(norm_mix_post[l]))
        x = _mlp(x.reshape(batch * seq, d), row(norm_mlp_pre[l]), w_up[l], w_down[l],
                 row(norm_mlp_post[l])).reshape(batch, seq, d)
    return x
```

```python
import functools

import jax
import jax.numpy as jnp
from jax import lax
from jax.experimental import pallas as pl
from jax.experimental.pallas import tpu as pltpu

D_MODEL = 1024
POOL_WIDTH = 512
POOL_GROUP_DIM = 128
POOL_WINDOWS = (2, 4, 8, 16)
CONV_WIDTH = 512
CONV_K = 31
D_FF = 4 * D_MODEL
EPS = 1e-6

SUBLANES = 8
LANES = 128
POOL_HALO = 16
CONV_HALO = 32
CONV_PITCH = CONV_WIDTH + LANES
GLU_OFF = POOL_WIDTH
GATE_A_OFF = POOL_WIDTH + 2 * CONV_WIDTH
_GLU_HALF = CONV_WIDTH // 2
W_IN_COL_STARTS = (
    list(range(0, GLU_OFF, LANES))
    + [GLU_OFF + part * CONV_WIDTH + j * _GLU_HALF + c
       for j in range(2) for part in range(2) for c in range(0, _GLU_HALF, LANES)]
    + list(range(GATE_A_OFF, GATE_A_OFF + 2 * D_MODEL, LANES)))

MIXER_TILE = 512
MLP_TILE = 1024
FF_CHUNK = 1024
PROJ_COLS = 512
CONV_ROWS = 64
PACKED_ROWS = 16
W_IN_CHUNK_ROWS = 128
W_OUT_CHUNK_ROWS = 256
W_UP_CHUNK_ROWS = 128
W_DOWN_CHUNK_ROWS = 512
VMEM_LIMIT_BYTES = 56 * 1024 * 1024


def _dot(a, b):
    return jnp.dot(a, b, preferred_element_type=jnp.float32)


def _rmsnorm(x, g):
    ms = jnp.mean(x * x, axis=-1, keepdims=True)
    return x * lax.rsqrt(ms + EPS) * g


def _zero_after(v):
    return jnp.minimum(jnp.abs(v), 0.0)


def _repeat(v, rows, cols):
    v = jnp.concatenate([v] * (cols // v.shape[1]), axis=1)
    return jnp.concatenate([v] * (rows // v.shape[0]), axis=0)


def _stream_rows(src_hbm, stage, sem, chunk_rows, consume):
    n_chunks = src_hbm.shape[0] // chunk_rows

    def copy(c):
        return pltpu.make_async_copy(src_hbm.at[pl.ds(c * chunk_rows, chunk_rows), :],
                                     stage.at[c % 2], sem.at[c % 2])

    copy(0).start()
    for c in range(n_chunks):
        if c + 1 < n_chunks:
            copy(c + 1).start()
        copy(c).wait()
        consume(c, stage[c % 2])


def _load_as_bf16(src_hbm, dst_ref, stage, sem, chunk_rows, col_starts=None):
    def cast(c, chunk):
        if col_starts is not None:
            chunk = jnp.concatenate([chunk[:, c0:c0 + LANES] for c0 in col_starts], axis=1)
        dst_ref[c * chunk_rows:(c + 1) * chunk_rows, :] = chunk.astype(jnp.bfloat16)

    _stream_rows(src_hbm, stage, sem, chunk_rows, cast)


def _load_folded_pool(w_pool_out_hbm, pool_w_ref, pool_scale_ref, dst_ref, stage, sem, chunk_rows):
    groups_per_chunk = chunk_rows // POOL_GROUP_DIM

    def fold(c, chunk):
        for j in range(groups_per_chunk):
            g = c * groups_per_chunk + j
            scaled = pool_w_ref[g] * pool_scale_ref[g]
            rows = chunk[j * POOL_GROUP_DIM:(j + 1) * POOL_GROUP_DIM, :]
            folded = jnp.dot(scaled, rows, precision=lax.Precision.HIGHEST,
                             preferred_element_type=jnp.float32)
            dst_ref[g * POOL_GROUP_DIM:(g + 1) * POOL_GROUP_DIM, :] = folded.astype(jnp.bfloat16)

    _stream_rows(w_pool_out_hbm, stage, sem, chunk_rows, fold)


def _mixer_kernel(x_ref, g_pre_ref, w_in_hbm, pool_w_ref, pool_scale_ref, w_pool_out_hbm,
                  conv_w_ref, conv_b_ref, ln_g_ref, ln_b_ref, w_conv_out_hbm, w_o_hbm,
                  g_post_ref, o_ref,
                  w_in_ref, w_pool_ref, w_conv_out_ref, w_o_ref, stage_in, stage_out, sem,
                  h_ref, pool_ext, pooled_ref, glu_ref, conv_x, conv_out, y_ref, gate_ref, merged_ref,
                  *, tile):
    t = pl.program_id(1)
    conv_rows = CONV_HALO + tile

    @pl.when((pl.program_id(0) == 0) & (t == 0))
    def _():
        _load_as_bf16(w_in_hbm, w_in_ref, stage_in, sem, W_IN_CHUNK_ROWS, col_starts=W_IN_COL_STARTS)
        _load_folded_pool(w_pool_out_hbm, pool_w_ref, pool_scale_ref, w_pool_ref, stage_out, sem,
                          W_OUT_CHUNK_ROWS)
        _load_as_bf16(w_conv_out_hbm, w_conv_out_ref, stage_out, sem, W_OUT_CHUNK_ROWS)
        _load_as_bf16(w_o_hbm, w_o_ref, stage_out, sem, W_OUT_CHUNK_ROWS)

    @pl.when(t == 0)
    def _():
        pool_ext[0:POOL_HALO, :] = jnp.zeros((POOL_HALO, POOL_WIDTH), jnp.float32)
        conv_x[0, 0:CONV_HALO, 0:CONV_WIDTH] = jnp.zeros((CONV_HALO, CONV_WIDTH), jnp.float32)

    def project(c0):
        return _dot(h_ref[:, 0:D_MODEL], w_in_ref[:, c0:c0 + PROJ_COLS])

    h_ref[:, 0:D_MODEL] = _rmsnorm(x_ref[...], g_pre_ref[...]).astype(jnp.bfloat16)

    half = CONV_WIDTH // 2
    for j in range(2):
        glu_ref[:, j * PROJ_COLS:(j + 1) * PROJ_COLS] = project(GLU_OFF + j * PROJ_COLS)
        for r0 in range(0, tile, CONV_ROWS):
            rows = slice(r0, r0 + CONV_ROWS)
            conv_x[0, CONV_HALO + r0:CONV_HALO + r0 + CONV_ROWS, j * half:(j + 1) * half] = (
                glu_ref[rows, j * PROJ_COLS:j * PROJ_COLS + half]
                * jax.nn.sigmoid(glu_ref[rows, j * PROJ_COLS + half:(j + 1) * PROJ_COLS]))
        if j == 0:
            pool_ext[POOL_HALO:POOL_HALO + tile, :] = project(0)
        for c0 in range(j * half, (j + 1) * half, LANES):
            cols = slice(c0, c0 + LANES)
            x0 = conv_x[0, :, cols]
            for s in range(1, SUBLANES):
                conv_x[s, :, cols] = pltpu.roll(x0, conv_rows - s, axis=0)

    frame = t * tile + lax.broadcasted_iota(jnp.int32, (tile, 1), 0)
    for g, window in enumerate(POOL_WINDOWS):
        cols = slice(g * POOL_GROUP_DIM, (g + 1) * POOL_GROUP_DIM)
        ext = pool_ext[:, cols]
        s = ext
        step = 1
        while step < window:
            s = s + pltpu.roll(s, step, axis=0)
            step *= 2
        inv_count = 1.0 / jnp.minimum(frame + 1, window).astype(jnp.float32)
        pooled = s[POOL_HALO:, :] * inv_count - ext[POOL_HALO:, :]
        pooled_ref[:, cols] = pooled.astype(jnp.bfloat16)

    first_tap = CONV_HALO - (CONV_K - 1)
    for r0 in range(0, tile, CONV_ROWS):
        accs = []
        for c0 in range(0, CONV_WIDTH, LANES):
            cols = slice(c0, c0 + LANES)
            acc = jnp.broadcast_to(conv_b_ref[:, cols], (CONV_ROWS, LANES))
            for k in range(CONV_K):
                q, s = divmod(first_tap + k, SUBLANES)
                lo = r0 + q * SUBLANES
                acc = acc + conv_w_ref[k:k + 1, cols] * conv_x[s, lo:lo + CONV_ROWS, cols]
            accs.append(acc)
        conv_out[r0:r0 + CONV_ROWS, :] = jnp.concatenate(accs, axis=-1)

    def gate_piece(i, after):
        lhs = h_ref[:, 0:D_MODEL]
        if after is not None:
            lhs = lhs + _repeat(_zero_after(after).astype(jnp.bfloat16), tile, D_MODEL)
        c0 = i * PROJ_COLS
        piece = jax.nn.sigmoid(_dot(lhs, w_in_ref[:, GATE_A_OFF + c0:GATE_A_OFF + c0 + PROJ_COLS]))
        gate_ref[:, c0:c0 + PROJ_COLS] = piece
        return piece[tile - PACKED_ROWS:, PROJ_COLS - LANES:]

    def norm_stage(i, after):
        for r0 in range(i * stage_rows, (i + 1) * stage_rows, CONV_ROWS):
            c = conv_out[r0:r0 + CONV_ROWS, :]
            if after is not None:
                c = c + _repeat(_zero_after(after), CONV_ROWS, CONV_WIDTH)
            mu = jnp.mean(c, axis=-1, keepdims=True)
            cc = c - mu
            var = jnp.mean(cc * cc, axis=-1, keepdims=True)
            y = cc * lax.rsqrt(var + EPS) * ln_g_ref[...] + ln_b_ref[...]
            y = y * jax.nn.sigmoid(y)
            y_ref[r0:r0 + CONV_ROWS, 0:CONV_WIDTH] = y.astype(jnp.bfloat16)
        return y[CONV_ROWS - PACKED_ROWS:, CONV_WIDTH - LANES:]

    n_pieces = 2 * D_MODEL // PROJ_COLS
    stage_rows = tile // n_pieces
    mxu_done = vpu_done = None
    for i in range(n_pieces):
        piece_done = gate_piece(i, vpu_done)
        vpu_done = norm_stage(i, mxu_done)
        mxu_done = piece_done

    for c0 in range(0, D_MODEL, PROJ_COLS):
        cols = slice(c0, c0 + PROJ_COLS)
        a = _dot(pooled_ref[:, 0:POOL_WIDTH], w_pool_ref[:, cols])
        b = _dot(y_ref[:, 0:CONV_WIDTH], w_conv_out_ref[:, cols])
        merged = (gate_ref[:, cols] * a
                  + gate_ref[:, D_MODEL + c0:D_MODEL + c0 + PROJ_COLS] * b)
        merged_ref[:, cols] = merged.astype(jnp.bfloat16)
    mix = _dot(merged_ref[:, 0:D_MODEL], w_o_ref[...])
    o_ref[...] = x_ref[...] + _rmsnorm(mix, g_post_ref[...])

    pool_ext[0:POOL_HALO, :] = pool_ext[tile:tile + POOL_HALO, :]
    conv_x[0, 0:CONV_HALO, 0:CONV_WIDTH] = conv_x[0, tile:tile + CONV_HALO, 0:CONV_WIDTH]


def _mlp_kernel(x_ref, g_pre_ref, w_up_hbm, w_down_hbm, g_post_ref, o_ref,
                w_up_ref, w_down_ref, stage_up, stage_down, sem):
    @pl.when(pl.program_id(0) == 0)
    def _():
        _load_as_bf16(w_up_hbm, w_up_ref, stage_up, sem, W_UP_CHUNK_ROWS)
        _load_as_bf16(w_down_hbm, w_down_ref, stage_down, sem, W_DOWN_CHUNK_ROWS)

    x = x_ref[...]
    h = _rmsnorm(x, g_pre_ref[...]).astype(jnp.bfloat16)
    acc = None
    for c0 in range(0, D_FF, FF_CHUNK):
        a = jnp.maximum(_dot(h, w_up_ref[:, c0:c0 + FF_CHUNK]), 0.0)
        p = _dot((a * a).astype(jnp.bfloat16), w_down_ref[c0:c0 + FF_CHUNK, :])
        acc = p if acc is None else acc + p
    o_ref[...] = x + _rmsnorm(acc, g_post_ref[...])


def _resident(shape):
    return pl.BlockSpec(shape, lambda *_: (0,) * len(shape), pipeline_mode=pl.Buffered(1))


def _mixer(x, g_pre, w_in, pool_w, pool_scale, w_pool_out, conv_w, layer, conv_b, ln_g, ln_b,
           w_conv_out, w_o, g_post):
    batch, seq, d = x.shape
    tile = MIXER_TILE
    assert seq % tile == 0 and d == D_MODEL
    x_spec = pl.BlockSpec((None, tile, d), lambda b, t: (b, t, 0))
    operands = (g_pre, w_in, pool_w, pool_scale, w_pool_out, conv_w, conv_b, ln_g, ln_b,
                w_conv_out, w_o, g_post)
    in_hbm = (w_in, w_pool_out, w_conv_out, w_o)
    hbm = pl.BlockSpec(memory_space=pl.ANY)
    conv_w_spec = pl.BlockSpec((None,) + conv_w.shape[1:], lambda *_: (layer, 0, 0),
                               pipeline_mode=pl.Buffered(1))

    def spec(a):
        if a is conv_w:
            return conv_w_spec
        return hbm if any(a is w for w in in_hbm) else _resident(a.shape)

    return pl.pallas_call(
        functools.partial(_mixer_kernel, tile=tile),
        grid=(batch, seq // tile),
        in_specs=[x_spec] + [spec(a) for a in operands],
        out_specs=x_spec,
        out_shape=jax.ShapeDtypeStruct(x.shape, x.dtype),
        scratch_shapes=[
            pltpu.VMEM(w_in.shape, jnp.bfloat16),
            pltpu.VMEM(w_pool_out.shape, jnp.bfloat16),
            pltpu.VMEM(w_conv_out.shape, jnp.bfloat16),
            pltpu.VMEM(w_o.shape, jnp.bfloat16),
            pltpu.VMEM((2, W_IN_CHUNK_ROWS, w_in.shape[1]), jnp.float32),
            pltpu.VMEM((2, W_OUT_CHUNK_ROWS, d), jnp.float32),
            pltpu.SemaphoreType.DMA((2,)),
            pltpu.VMEM((tile, d + LANES), jnp.bfloat16),
            pltpu.VMEM((POOL_HALO + tile, POOL_WIDTH), jnp.float32),
            pltpu.VMEM((tile, POOL_WIDTH + LANES), jnp.bfloat16),
            pltpu.VMEM((tile, 2 * CONV_WIDTH), jnp.float32),
            pltpu.VMEM((SUBLANES, CONV_HALO + tile, CONV_PITCH), jnp.float32),
            pltpu.VMEM((tile, CONV_WIDTH), jnp.float32),
            pltpu.VMEM((tile, CONV_WIDTH + LANES), jnp.bfloat16),
            pltpu.VMEM((tile, 2 * d), jnp.float32),
            pltpu.VMEM((tile, d + LANES), jnp.bfloat16),
        ],
        compiler_params=pltpu.CompilerParams(
            dimension_semantics=("arbitrary", "arbitrary"),
            vmem_limit_bytes=VMEM_LIMIT_BYTES),
        name="mixer",
    )(x, *operands)


def _mlp(x, g_pre, w_up, w_down, g_post):
    tokens, d = x.shape
    tile = MLP_TILE
    assert tokens % tile == 0 and d == D_MODEL
    x_spec = pl.BlockSpec((tile, d), lambda i: (i, 0))
    operands = (g_pre, w_up, w_down, g_post)
    return pl.pallas_call(
        _mlp_kernel,
        grid=(tokens // tile,),
        in_specs=[x_spec, _resident(g_pre.shape), pl.BlockSpec(memory_space=pl.ANY),
                  pl.BlockSpec(memory_space=pl.ANY), _resident(g_post.shape)],
        out_specs=x_spec,
        out_shape=jax.ShapeDtypeStruct(x.shape, x.dtype),
        scratch_shapes=[
            pltpu.VMEM(w_up.shape, jnp.bfloat16),
            pltpu.VMEM(w_down.shape, jnp.bfloat16),
            pltpu.VMEM((2, W_UP_CHUNK_ROWS, w_up.shape[1]), jnp.float32),
            pltpu.VMEM((2, W_DOWN_CHUNK_ROWS, w_down.shape[1]), jnp.float32),
            pltpu.SemaphoreType.DMA((2,)),
        ],
        compiler_params=pltpu.CompilerParams(
            dimension_semantics=("arbitrary",),
            vmem_limit_bytes=VMEM_LIMIT_BYTES),
        name="mlp",
    )(x, *operands)


def kernel(x, norm_mix_pre, w_in, pool_w, pool_scale, w_pool_out, conv_w, conv_b, conv_ln_g,
           conv_ln_b, w_conv_out, w_o, norm_mix_post, norm_mlp_pre, w_up, w_down, norm_mlp_post):
    batch, seq, d = x.shape
    row = lambda p: p.reshape(1, -1)
    for l in range(w_in.shape[0]):
        x = _mixer(x, row(norm_mix_pre[l]), w_in[l], pool_w[l],
                   pool_scale[l].reshape(len(POOL_WINDOWS), 1, POOL_GROUP_DIM), w_pool_out[l],
                   conv_w, l, row(conv_b[l]), row(conv_ln_g[l]), row(conv_ln_b[l]), w_conv_out[l],
                   w_o[l], row(norm_mix_post[l]))
        x = _mlp(x.reshape(batch * seq, d), row(norm_mlp_pre[l]), w_up[l], w_down[l],
                 row(norm_mlp_post[l])).reshape(batch, seq, d)
    return x
```

```python
import functools

import jax
import jax.numpy as jnp
from jax import lax
from jax.experimental import pallas as pl
from jax.experimental.pallas import tpu as pltpu

D_MODEL = 1024
POOL_WIDTH = 512
POOL_GROUP_DIM = 128
POOL_WINDOWS = (2, 4, 8, 16)
CONV_WIDTH = 512
CONV_K = 31
D_FF = 4 * D_MODEL
EPS = 1e-6

SUBLANES = 8
LANES = 128
POOL_HALO = 16
CONV_HALO = 32
CONV_PITCH = CONV_WIDTH + LANES
GLU_OFF = POOL_WIDTH
GATE_A_OFF = POOL_WIDTH + 2 * CONV_WIDTH
_GLU_HALF = CONV_WIDTH // 2
W_IN_COL_STARTS = (
    list(range(0, GLU_OFF, LANES))
    + [GLU_OFF + part * CONV_WIDTH + j * _GLU_HALF + c
       for j in range(2) for part in range(2) for c in range(0, _GLU_HALF, LANES)]
    + list(range(GATE_A_OFF, GATE_A_OFF + 2 * D_MODEL, LANES)))

MIXER_TILE = 512
MLP_TILE = 1024
FF_CHUNK = 1024
PROJ_COLS = 512
CONV_ROWS = 64
PACKED_ROWS = 16
W_IN_CHUNK_ROWS = 128
W_OUT_CHUNK_ROWS = 256
W_UP_CHUNK_ROWS = 128
W_DOWN_CHUNK_ROWS = 512
VMEM_LIMIT_BYTES = 56 * 1024 * 1024


def _dot(a, b):
    return jnp.dot(a, b, preferred_element_type=jnp.float32)


def _rmsnorm(x, g):
    ms = jnp.mean(x * x, axis=-1, keepdims=True)
    return x * lax.rsqrt(ms + EPS) * g


def _zero_after(v):
    return jnp.minimum(jnp.abs(v), 0.0)


def _repeat(v, rows, cols):
    v = jnp.concatenate([v] * (cols // v.shape[1]), axis=1)
    return jnp.concatenate([v] * (rows // v.shape[0]), axis=0)


def _stream_rows(src_hbm, stage, sem, chunk_rows, consume):
    n_chunks = src_hbm.shape[0] // chunk_rows

    def copy(c):
        return pltpu.make_async_copy(src_hbm.at[pl.ds(c * chunk_rows, chunk_rows), :],
                                     stage.at[c % 2], sem.at[c % 2])

    copy(0).start()
    for c in range(n_chunks):
        if c + 1 < n_chunks:
            copy(c + 1).start()
        copy(c).wait()
        consume(c, stage[c % 2])


def _load_as_bf16(src_hbm, dst_ref, stage, sem, chunk_rows, col_starts=None):
    def cast(c, chunk):
        if col_starts is not None:
            chunk = jnp.concatenate([chunk[:, c0:c0 + LANES] for c0 in col_starts], axis=1)
        dst_ref[c * chunk_rows:(c + 1) * chunk_rows, :] = chunk.astype(jnp.bfloat16)

    _stream_rows(src_hbm, stage, sem, chunk_rows, cast)


def _load_folded_pool(w_pool_out_hbm, pool_w_ref, pool_scale_ref, dst_ref, stage, sem, chunk_rows):
    groups_per_chunk = chunk_rows // POOL_GROUP_DIM

    def fold(c, chunk):
        for j in range(groups_per_chunk):
            g = c * groups_per_chunk + j
            scaled = pool_w_ref[g] * pool_scale_ref[g]
            rows = chunk[j * POOL_GROUP_DIM:(j + 1) * POOL_GROUP_DIM, :]
            folded = jnp.dot(scaled, rows, precision=lax.Precision.HIGHEST,
                             preferred_element_type=jnp.float32)
            dst_ref[g * POOL_GROUP_DIM:(g + 1) * POOL_GROUP_DIM, :] = folded.astype(jnp.bfloat16)

    _stream_rows(w_pool_out_hbm, stage, sem, chunk_rows, fold)


def _mixer_kernel(x_ref, g_pre_ref, w_in_hbm, pool_w_ref, pool_scale_ref, w_pool_out_hbm,
                  conv_w_ref, conv_b_ref, ln_g_ref, ln_b_ref, w_conv_out_hbm, w_o_hbm,
                  g_post_ref, o_ref,
                  w_in_ref, w_pool_ref, w_conv_out_ref, w_o_ref, stage_in, stage_out, sem,
                  h_ref, pool_ext, pooled_ref, glu_ref, conv_x, conv_out, y_ref, gate_ref, merged_ref,
                  *, tile):
    t = pl.program_id(1)
    conv_rows = CONV_HALO + tile

    @pl.when((pl.program_id(0) == 0) & (t == 0))
    def _():
        _load_as_bf16(w_in_hbm, w_in_ref, stage_in, sem, W_IN_CHUNK_ROWS, col_starts=W_IN_COL_STARTS)
        _load_folded_pool(w_pool_out_hbm, pool_w_ref, pool_scale_ref, w_pool_ref, stage_out, sem,
                          W_OUT_CHUNK_ROWS)
        _load_as_bf16(w_conv_out_hbm, w_conv_out_ref, stage_out, sem, W_OUT_CHUNK_ROWS)
        _load_as_bf16(w_o_hbm, w_o_ref, stage_out, sem, W_OUT_CHUNK_ROWS)

    @pl.when(t == 0)
    def _():
        pool_ext[0:POOL_HALO, :] = jnp.zeros((POOL_HALO, POOL_WIDTH), jnp.float32)
        conv_x[0, 0:CONV_HALO, 0:CONV_WIDTH] = jnp.zeros((CONV_HALO, CONV_WIDTH), jnp.float32)

    def project(c0):
        return _dot(h_ref[...], w_in_ref[:, c0:c0 + PROJ_COLS])

    h_ref[...] = _rmsnorm(x_ref[...], g_pre_ref[...]).astype(jnp.bfloat16)

    half = CONV_WIDTH // 2
    for j in range(2):
        glu_ref[:, j * PROJ_COLS:(j + 1) * PROJ_COLS] = project(GLU_OFF + j * PROJ_COLS)
        for r0 in range(0, tile, CONV_ROWS):
            rows = slice(r0, r0 + CONV_ROWS)
            conv_x[0, CONV_HALO + r0:CONV_HALO + r0 + CONV_ROWS, j * half:(j + 1) * half] = (
                glu_ref[rows, j * PROJ_COLS:j * PROJ_COLS + half]
                * jax.nn.sigmoid(glu_ref[rows, j * PROJ_COLS + half:(j + 1) * PROJ_COLS]))
        if j == 0:
            pool_ext[POOL_HALO:POOL_HALO + tile, :] = project(0)
        for c0 in range(j * half, (j + 1) * half, LANES):
            cols = slice(c0, c0 + LANES)
            x0 = conv_x[0, :, cols]
            for s in range(1, SUBLANES):
                conv_x[s, :, cols] = pltpu.roll(x0, conv_rows - s, axis=0)

    frame = t * tile + lax.broadcasted_iota(jnp.int32, (tile, 1), 0)
    for g, window in enumerate(POOL_WINDOWS):
        cols = slice(g * POOL_GROUP_DIM, (g + 1) * POOL_GROUP_DIM)
        ext = pool_ext[:, cols]
        s = ext
        step = 1
        while step < window:
            s = s + pltpu.roll(s, step, axis=0)
            step *= 2
        inv_count = 1.0 / jnp.minimum(frame + 1, window).astype(jnp.float32)
        pooled = s[POOL_HALO:, :] * inv_count - ext[POOL_HALO:, :]
        pooled_ref[:, cols] = pooled.astype(jnp.bfloat16)

    first_tap = CONV_HALO - (CONV_K - 1)
    for r0 in range(0, tile, CONV_ROWS):
        accs = []
        for c0 in range(0, CONV_WIDTH, LANES):
            cols = slice(c0, c0 + LANES)
            acc = jnp.broadcast_to(conv_b_ref[:, cols], (CONV_ROWS, LANES))
            for k in range(CONV_K):
                q, s = divmod(first_tap + k, SUBLANES)
                lo = r0 + q * SUBLANES
                acc = acc + conv_w_ref[k:k + 1, cols] * conv_x[s, lo:lo + CONV_ROWS, cols]
            accs.append(acc)
        conv_out[r0:r0 + CONV_ROWS, :] = jnp.concatenate(accs, axis=-1)

    def gate_piece(i, after):
        lhs = h_ref[...]
        if after is not None:
            lhs = lhs + _repeat(_zero_after(after).astype(jnp.bfloat16), tile, D_MODEL)
        c0 = i * PROJ_COLS
        piece = jax.nn.sigmoid(_dot(lhs, w_in_ref[:, GATE_A_OFF + c0:GATE_A_OFF + c0 + PROJ_COLS]))
        gate_ref[:, c0:c0 + PROJ_COLS] = piece
        return piece[tile - PACKED_ROWS:, PROJ_COLS - LANES:]

    def norm_stage(i, after):
        for r0 in range(i * stage_rows, (i + 1) * stage_rows, CONV_ROWS):
            c = conv_out[r0:r0 + CONV_ROWS, :]
            if after is not None:
                c = c + _repeat(_zero_after(after), CONV_ROWS, CONV_WIDTH)
            mu = jnp.mean(c, axis=-1, keepdims=True)
            cc = c - mu
            var = jnp.mean(cc * cc, axis=-1, keepdims=True)
            y = cc * lax.rsqrt(var + EPS) * ln_g_ref[...] + ln_b_ref[...]
            y = y * jax.nn.sigmoid(y)
            y_ref[r0:r0 + CONV_ROWS, 0:CONV_WIDTH] = y.astype(jnp.bfloat16)
        return y[CONV_ROWS - PACKED_ROWS:, CONV_WIDTH - LANES:]

    n_pieces = 2 * D_MODEL // PROJ_COLS
    stage_rows = tile // n_pieces
    mxu_done = vpu_done = None
    for i in range(n_pieces):
        piece_done = gate_piece(i, vpu_done)
        vpu_done = norm_stage(i, mxu_done)
        mxu_done = piece_done

    for c0 in range(0, D_MODEL, PROJ_COLS):
        cols = slice(c0, c0 + PROJ_COLS)
        a = _dot(pooled_ref[:, 0:POOL_WIDTH], w_pool_ref[:, cols])
        b = _dot(y_ref[:, 0:CONV_WIDTH], w_conv_out_ref[:, cols])
        merged = (gate_ref[:, cols] * a
                  + gate_ref[:, D_MODEL + c0:D_MODEL + c0 + PROJ_COLS] * b)
        merged_ref[:, cols] = merged.astype(jnp.bfloat16)
    mix = _dot(merged_ref[:, 0:D_MODEL], w_o_ref[...])
    o_ref[...] = x_ref[...] + _rmsnorm(mix, g_post_ref[...])

    pool_ext[0:POOL_HALO, :] = pool_ext[tile:tile + POOL_HALO, :]
    conv_x[0, 0:CONV_HALO, 0:CONV_WIDTH] = conv_x[0, tile:tile + CONV_HALO, 0:CONV_WIDTH]


def _mlp_kernel(x_ref, g_pre_ref, w_up_hbm, w_down_hbm, g_post_ref, o_ref,
                w_up_ref, w_down_ref, stage_up, stage_down, sem):
    @pl.when(pl.program_id(0) == 0)
    def _():
        _load_as_bf16(w_up_hbm, w_up_ref, stage_up, sem, W_UP_CHUNK_ROWS)
        _load_as_bf16(w_down_hbm, w_down_ref, stage_down, sem, W_DOWN_CHUNK_ROWS)

    x = x_ref[...]
    h = _rmsnorm(x, g_pre_ref[...]).astype(jnp.bfloat16)
    acc = None
    for c0 in range(0, D_FF, FF_CHUNK):
        a = jnp.maximum(_dot(h, w_up_ref[:, c0:c0 + FF_CHUNK]), 0.0)
        p = _dot((a * a).astype(jnp.bfloat16), w_down_ref[c0:c0 + FF_CHUNK, :])
        acc = p if acc is None else acc + p
    o_ref[...] = x + _rmsnorm(acc, g_post_ref[...])


def _resident(shape):
    return pl.BlockSpec(shape, lambda *_: (0,) * len(shape), pipeline_mode=pl.Buffered(1))


def _mixer(x, g_pre, w_in, pool_w, pool_scale, w_pool_out, conv_w, layer, conv_b, ln_g, ln_b,
           w_conv_out, w_o, g_post):
    batch, seq, d = x.shape
    tile = MIXER_TILE
    assert seq % tile == 0 and d == D_MODEL
    x_spec = pl.BlockSpec((None, tile, d), lambda b, t: (b, t, 0))
    operands = (g_pre, w_in, pool_w, pool_scale, w_pool_out, conv_w, conv_b, ln_g, ln_b,
                w_conv_out, w_o, g_post)
    in_hbm = (w_in, w_pool_out, w_conv_out, w_o)
    hbm = pl.BlockSpec(memory_space=pl.ANY)
    conv_w_spec = pl.BlockSpec((None,) + conv_w.shape[1:], lambda *_: (layer, 0, 0),
                               pipeline_mode=pl.Buffered(1))

    def spec(a):
        if a is conv_w:
            return conv_w_spec
        return hbm if any(a is w for w in in_hbm) else _resident(a.shape)

    return pl.pallas_call(
        functools.partial(_mixer_kernel, tile=tile),
        grid=(batch, seq // tile),
        in_specs=[x_spec] + [spec(a) for a in operands],
        out_specs=x_spec,
        out_shape=jax.ShapeDtypeStruct(x.shape, x.dtype),
        scratch_shapes=[
            pltpu.VMEM(w_in.shape, jnp.bfloat16),
            pltpu.VMEM(w_pool_out.shape, jnp.bfloat16),
            pltpu.VMEM(w_conv_out.shape, jnp.bfloat16),
            pltpu.VMEM(w_o.shape, jnp.bfloat16),
            pltpu.VMEM((2, W_IN_CHUNK_ROWS, w_in.shape[1]), jnp.float32),
            pltpu.VMEM((2, W_OUT_CHUNK_ROWS, d), jnp.float32),
            pltpu.SemaphoreType.DMA((2,)),
            pltpu.VMEM((tile, d), jnp.bfloat16),
            pltpu.VMEM((POOL_HALO + tile, POOL_WIDTH), jnp.float32),
            pltpu.VMEM((tile, POOL_WIDTH + LANES), jnp.bfloat16),
            pltpu.VMEM((tile, 2 * CONV_WIDTH), jnp.float32),
            pltpu.VMEM((SUBLANES, CONV_HALO + tile, CONV_PITCH), jnp.float32),
            pltpu.VMEM((tile, CONV_WIDTH), jnp.float32),
            pltpu.VMEM((tile, CONV_WIDTH + LANES), jnp.bfloat16),
            pltpu.VMEM((tile, 2 * d), jnp.float32),
            pltpu.VMEM((tile, d + LANES), jnp.bfloat16),
        ],
        compiler_params=pltpu.CompilerParams(
            dimension_semantics=("arbitrary", "arbitrary"),
            vmem_limit_bytes=VMEM_LIMIT_BYTES),
        name="mixer",
    )(x, *operands)


def _mlp(x, g_pre, w_up, w_down, g_post):
    tokens, d = x.shape
    tile = MLP_TILE
    assert tokens % tile == 0 and d == D_MODEL
    x_spec = pl.BlockSpec((tile, d), lambda i: (i, 0))
    operands = (g_pre, w_up, w_down, g_post)
    return pl.pallas_call(
        _mlp_kernel,
        grid=(tokens // tile,),
        in_specs=[x_spec, _resident(g_pre.shape), pl.BlockSpec(memory_space=pl.ANY),
                  pl.BlockSpec(memory_space=pl.ANY), _resident(g_post.shape)],
        out_specs=x_spec,
        out_shape=jax.ShapeDtypeStruct(x.shape, x.dtype),
        scratch_shapes=[
            pltpu.VMEM(w_up.shape, jnp.bfloat16),
            pltpu.VMEM(w_down.shape, jnp.bfloat16),
            pltpu.VMEM((2, W_UP_CHUNK_ROWS, w_up.shape[1]), jnp.float32),
            pltpu.VMEM((2, W_DOWN_CHUNK_ROWS, w_down.shape[1]), jnp.float32),
            pltpu.SemaphoreType.DMA((2,)),
        ],
        compiler_params=pltpu.CompilerParams(
            dimension_semantics=("arbitrary",),
            vmem_limit_bytes=VMEM_LIMIT_BYTES),
        name="mlp",
    )(x, *operands)


def kernel(x, norm_mix_pre, w_in, pool_w, pool_scale, w_pool_out, conv_w, conv_b, conv_ln_g,
           conv_ln_b, w_conv_out, w_o, norm_mix_post, norm_mlp_pre, w_up, w_down, norm_mlp_post):
    batch, seq, d = x.shape
    row = lambda p: p.reshape(1, -1)
    for l in range(w_in.shape[0]):
        x = _mixer(x, row(norm_mix_pre[l]), w_in[l], pool_w[l],
                   pool_scale[l].reshape(len(POOL_WINDOWS), 1, POOL_GROUP_DIM), w_pool_out[l],
                   conv_w, l, row(conv_b[l]), row(conv_ln_g[l]), row(conv_ln_b[l]), w_conv_out[l],
                   w_o[l], row(norm_mix_post[l]))
        x = _mlp(x.reshape(batch * seq, d), row(norm_mlp_pre[l]), w_up[l], w_down[l],
                 row(norm_mlp_post[l])).reshape(batch, seq, d)
    return x
```

```python
import functools

import jax
import jax.numpy as jnp
from jax import lax
from jax.experimental import pallas as pl
from jax.experimental.pallas import tpu as pltpu

D_MODEL = 1024
POOL_WIDTH = 512
POOL_GROUP_DIM = 128
POOL_WINDOWS = (2, 4, 8, 16)
CONV_WIDTH = 512
CONV_K = 31
D_FF = 4 * D_MODEL
EPS = 1e-6

SUBLANES = 8
LANES = 128
POOL_HALO = 16
CONV_HALO = 32
CONV_PITCH = CONV_WIDTH + LANES
GLU_OFF = POOL_WIDTH
GATE_A_OFF = POOL_WIDTH + 2 * CONV_WIDTH
_GLU_HALF = CONV_WIDTH // 2
W_IN_COL_STARTS = (
    list(range(0, GLU_OFF, LANES))
    + [GLU_OFF + part * CONV_WIDTH + j * _GLU_HALF + c
       for j in range(2) for part in range(2) for c in range(0, _GLU_HALF, LANES)]
    + list(range(GATE_A_OFF, GATE_A_OFF + 2 * D_MODEL, LANES)))

MIXER_TILE = 512
MLP_TILE = 1024
FF_CHUNK = 1024
PROJ_COLS = 512
CONV_ROWS = 64
PACKED_ROWS = 16
W_IN_CHUNK_ROWS = 128
W_OUT_CHUNK_ROWS = 256
W_UP_CHUNK_ROWS = 128
W_DOWN_CHUNK_ROWS = 512
VMEM_LIMIT_BYTES = 56 * 1024 * 1024


def _dot(a, b):
    return jnp.dot(a, b, preferred_element_type=jnp.float32)


def _rmsnorm(x, g):
    ms = jnp.mean(x * x, axis=-1, keepdims=True)
    return x * lax.rsqrt(ms + EPS) * g


def _zero_after(v):
    return jnp.minimum(jnp.abs(v), 0.0)


def _repeat(v, rows, cols):
    v = jnp.concatenate([v] * (cols // v.shape[1]), axis=1)
    return jnp.concatenate([v] * (rows // v.shape[0]), axis=0)


def _stream_rows(src_hbm, stage, sem, chunk_rows, consume):
    n_chunks = src_hbm.shape[0] // chunk_rows

    def copy(c):
        return pltpu.make_async_copy(src_hbm.at[pl.ds(c * chunk_rows, chunk_rows), :],
                                     stage.at[c % 2], sem.at[c % 2])

    copy(0).start()
    for c in range(n_chunks):
        if c + 1 < n_chunks:
            copy(c + 1).start()
        copy(c).wait()
        consume(c, stage[c % 2])


def _load_as_bf16(src_hbm, dst_ref, stage, sem, chunk_rows, col_starts=None):
    def cast(c, chunk):
        if col_starts is not None:
            chunk = jnp.concatenate([chunk[:, c0:c0 + LANES] for c0 in col_starts], axis=1)
        dst_ref[c * chunk_rows:(c + 1) * chunk_rows, 0:chunk.shape[1]] = chunk.astype(jnp.bfloat16)

    _stream_rows(src_hbm, stage, sem, chunk_rows, cast)


def _load_folded_pool(w_pool_out_hbm, pool_w_ref, pool_scale_ref, dst_ref, stage, sem, chunk_rows):
    groups_per_chunk = chunk_rows // POOL_GROUP_DIM

    def fold(c, chunk):
        for j in range(groups_per_chunk):
            g = c * groups_per_chunk + j
            scaled = pool_w_ref[g] * pool_scale_ref[g]
            rows = chunk[j * POOL_GROUP_DIM:(j + 1) * POOL_GROUP_DIM, :]
            folded = jnp.dot(scaled, rows, precision=lax.Precision.HIGHEST,
                             preferred_element_type=jnp.float32)
            dst_ref[g * POOL_GROUP_DIM:(g + 1) * POOL_GROUP_DIM, :] = folded.astype(jnp.bfloat16)

    _stream_rows(w_pool_out_hbm, stage, sem, chunk_rows, fold)


def _mixer_kernel(x_ref, g_pre_ref, w_in_hbm, pool_w_ref, pool_scale_ref, w_pool_out_hbm,
                  conv_w_ref, conv_b_ref, ln_g_ref, ln_b_ref, w_conv_out_hbm, w_o_hbm,
                  g_post_ref, o_ref,
                  w_in_ref, w_pool_ref, w_conv_out_ref, w_o_ref, stage_in, stage_out, sem,
                  h_ref, pool_ext, pooled_ref, glu_ref, conv_x, conv_out, y_ref, gate_ref, merged_ref,
                  *, tile):
    t = pl.program_id(1)
    conv_rows = CONV_HALO + tile

    @pl.when((pl.program_id(0) == 0) & (t == 0))
    def _():
        _load_as_bf16(w_in_hbm, w_in_ref, stage_in, sem, W_IN_CHUNK_ROWS, col_starts=W_IN_COL_STARTS)
        _load_folded_pool(w_pool_out_hbm, pool_w_ref, pool_scale_ref, w_pool_ref, stage_out, sem,
                          W_OUT_CHUNK_ROWS)
        _load_as_bf16(w_conv_out_hbm, w_conv_out_ref, stage_out, sem, W_OUT_CHUNK_ROWS)
        _load_as_bf16(w_o_hbm, w_o_ref, stage_out, sem, W_OUT_CHUNK_ROWS)

    @pl.when(t == 0)
    def _():
        pool_ext[0:POOL_HALO, :] = jnp.zeros((POOL_HALO, POOL_WIDTH), jnp.float32)
        conv_x[0, 0:CONV_HALO, 0:CONV_WIDTH] = jnp.zeros((CONV_HALO, CONV_WIDTH), jnp.float32)

    def project(c0):
        return _dot(h_ref[...], w_in_ref[:, c0:c0 + PROJ_COLS])

    h_ref[...] = _rmsnorm(x_ref[...], g_pre_ref[...]).astype(jnp.bfloat16)

    half = CONV_WIDTH // 2
    for j in range(2):
        glu_ref[:, j * PROJ_COLS:(j + 1) * PROJ_COLS] = project(GLU_OFF + j * PROJ_COLS)
        for r0 in range(0, tile, CONV_ROWS):
            rows = slice(r0, r0 + CONV_ROWS)
            conv_x[0, CONV_HALO + r0:CONV_HALO + r0 + CONV_ROWS, j * half:(j + 1) * half] = (
                glu_ref[rows, j * PROJ_COLS:j * PROJ_COLS + half]
                * jax.nn.sigmoid(glu_ref[rows, j * PROJ_COLS + half:(j + 1) * PROJ_COLS]))
        if j == 0:
            pool_ext[POOL_HALO:POOL_HALO + tile, :] = project(0)
        for c0 in range(j * half, (j + 1) * half, LANES):
            cols = slice(c0, c0 + LANES)
            x0 = conv_x[0, :, cols]
            for s in range(1, SUBLANES):
                conv_x[s, :, cols] = pltpu.roll(x0, conv_rows - s, axis=0)

    frame = t * tile + lax.broadcasted_iota(jnp.int32, (tile, 1), 0)
    for g, window in enumerate(POOL_WINDOWS):
        cols = slice(g * POOL_GROUP_DIM, (g + 1) * POOL_GROUP_DIM)
        ext = pool_ext[:, cols]
        s = ext
        step = 1
        while step < window:
            s = s + pltpu.roll(s, step, axis=0)
            step *= 2
        inv_count = 1.0 / jnp.minimum(frame + 1, window).astype(jnp.float32)
        pooled = s[POOL_HALO:, :] * inv_count - ext[POOL_HALO:, :]
        pooled_ref[:, cols] = pooled.astype(jnp.bfloat16)

    first_tap = CONV_HALO - (CONV_K - 1)
    for r0 in range(0, tile, CONV_ROWS):
        accs = []
        for c0 in range(0, CONV_WIDTH, LANES):
            cols = slice(c0, c0 + LANES)
            acc = jnp.broadcast_to(conv_b_ref[:, cols], (CONV_ROWS, LANES))
            for k in range(CONV_K):
                q, s = divmod(first_tap + k, SUBLANES)
                lo = r0 + q * SUBLANES
                acc = acc + conv_w_ref[k:k + 1, cols] * conv_x[s, lo:lo + CONV_ROWS, cols]
            accs.append(acc)
        conv_out[r0:r0 + CONV_ROWS, :] = jnp.concatenate(accs, axis=-1)

    def gate_piece(i, after):
        lhs = h_ref[...]
        if after is not None:
            lhs = lhs + _repeat(_zero_after(after).astype(jnp.bfloat16), tile, D_MODEL)
        c0 = i * PROJ_COLS
        piece = jax.nn.sigmoid(_dot(lhs, w_in_ref[:, GATE_A_OFF + c0:GATE_A_OFF + c0 + PROJ_COLS]))
        gate_ref[:, c0:c0 + PROJ_COLS] = piece
        return piece[tile - PACKED_ROWS:, PROJ_COLS - LANES:]

    def norm_stage(i, after):
        for r0 in range(i * stage_rows, (i + 1) * stage_rows, CONV_ROWS):
            c = conv_out[r0:r0 + CONV_ROWS, :]
            if after is not None:
                c = c + _repeat(_zero_after(after), CONV_ROWS, CONV_WIDTH)
            mu = jnp.mean(c, axis=-1, keepdims=True)
            cc = c - mu
            var = jnp.mean(cc * cc, axis=-1, keepdims=True)
            y = cc * lax.rsqrt(var + EPS) * ln_g_ref[...] + ln_b_ref[...]
            y = y * jax.nn.sigmoid(y)
            y_ref[r0:r0 + CONV_ROWS, :] = y.astype(jnp.bfloat16)
        return y[CONV_ROWS - PACKED_ROWS:, CONV_WIDTH - LANES:]

    n_pieces = 2 * D_MODEL // PROJ_COLS
    stage_rows = tile // n_pieces
    mxu_done = vpu_done = None
    for i in range(n_pieces):
        piece_done = gate_piece(i, vpu_done)
        vpu_done = norm_stage(i, mxu_done)
        mxu_done = piece_done

    for c0 in range(0, D_MODEL, PROJ_COLS):
        cols = slice(c0, c0 + PROJ_COLS)
        a = _dot(pooled_ref[...], w_pool_ref[:, cols])
        b = _dot(y_ref[...], w_conv_out_ref[:, cols])
        merged = (gate_ref[:, cols] * a
                  + gate_ref[:, D_MODEL + c0:D_MODEL + c0 + PROJ_COLS] * b)
        merged_ref[:, cols] = merged.astype(jnp.bfloat16)
    mix = _dot(merged_ref[...], w_o_ref[...])
    o_ref[...] = x_ref[...] + _rmsnorm(mix, g_post_ref[...])

    pool_ext[0:POOL_HALO, :] = pool_ext[tile:tile + POOL_HALO, :]
    conv_x[0, 0:CONV_HALO, 0:CONV_WIDTH] = conv_x[0, tile:tile + CONV_HALO, 0:CONV_WIDTH]


def _mlp_kernel(x_ref, g_pre_ref, w_up_hbm, w_down_hbm, g_post_ref, o_ref,
                w_up_ref, w_down_ref, stage_up, stage_down, sem):
    @pl.when(pl.program_id(0) == 0)
    def _():
        _load_as_bf16(w_up_hbm, w_up_ref, stage_up, sem, W_UP_CHUNK_ROWS)
        _load_as_bf16(w_down_hbm, w_down_ref, stage_down, sem, W_DOWN_CHUNK_ROWS)

    x = x_ref[...]
    h = _rmsnorm(x, g_pre_ref[...]).astype(jnp.bfloat16)
    acc = None
    for c0 in range(0, D_FF, FF_CHUNK):
        a = jnp.maximum(_dot(h, w_up_ref[:, c0:c0 + FF_CHUNK]), 0.0)
        p = _dot((a * a).astype(jnp.bfloat16), w_down_ref[c0:c0 + FF_CHUNK, 0:D_MODEL])
        acc = p if acc is None else acc + p
    o_ref[...] = x + _rmsnorm(acc, g_post_ref[...])


def _resident(shape):
    return pl.BlockSpec(shape, lambda *_: (0,) * len(shape), pipeline_mode=pl.Buffered(1))


def _mixer(x, g_pre, w_in, pool_w, pool_scale, w_pool_out, conv_w, layer, conv_b, ln_g, ln_b,
           w_conv_out, w_o, g_post):
    batch, seq, d = x.shape
    tile = MIXER_TILE
    assert seq % tile == 0 and d == D_MODEL
    x_spec = pl.BlockSpec((None, tile, d), lambda b, t: (b, t, 0))
    operands = (g_pre, w_in, pool_w, pool_scale, w_pool_out, conv_w, conv_b, ln_g, ln_b,
                w_conv_out, w_o, g_post)
    in_hbm = (w_in, w_pool_out, w_conv_out, w_o)
    hbm = pl.BlockSpec(memory_space=pl.ANY)
    conv_w_spec = pl.BlockSpec((None,) + conv_w.shape[1:], lambda *_: (layer, 0, 0),
                               pipeline_mode=pl.Buffered(1))

    def spec(a):
        if a is conv_w:
            return conv_w_spec
        return hbm if any(a is w for w in in_hbm) else _resident(a.shape)

    return pl.pallas_call(
        functools.partial(_mixer_kernel, tile=tile),
        grid=(batch, seq // tile),
        in_specs=[x_spec] + [spec(a) for a in operands],
        out_specs=x_spec,
        out_shape=jax.ShapeDtypeStruct(x.shape, x.dtype),
        scratch_shapes=[
            pltpu.VMEM(w_in.shape, jnp.bfloat16),
            pltpu.VMEM(w_pool_out.shape, jnp.bfloat16),
            pltpu.VMEM(w_conv_out.shape, jnp.bfloat16),
            pltpu.VMEM(w_o.shape, jnp.bfloat16),
            pltpu.VMEM((2, W_IN_CHUNK_ROWS, w_in.shape[1]), jnp.float32),
            pltpu.VMEM((2, W_OUT_CHUNK_ROWS, d), jnp.float32),
            pltpu.SemaphoreType.DMA((2,)),
            pltpu.VMEM((tile, d), jnp.bfloat16),
            pltpu.VMEM((POOL_HALO + tile, POOL_WIDTH), jnp.float32),
            pltpu.VMEM((tile, POOL_WIDTH), jnp.bfloat16),
            pltpu.VMEM((tile, 2 * CONV_WIDTH), jnp.float32),
            pltpu.VMEM((SUBLANES, CONV_HALO + tile, CONV_PITCH), jnp.float32),
            pltpu.VMEM((tile, CONV_WIDTH), jnp.float32),
            pltpu.VMEM((tile, CONV_WIDTH), jnp.bfloat16),
            pltpu.VMEM((tile, 2 * d), jnp.float32),
            pltpu.VMEM((tile, d), jnp.bfloat16),
        ],
        compiler_params=pltpu.CompilerParams(
            dimension_semantics=("arbitrary", "arbitrary"),
            vmem_limit_bytes=VMEM_LIMIT_BYTES),
        name="mixer",
    )(x, *operands)


def _mlp(x, g_pre, w_up, w_down, g_post):
    tokens, d = x.shape
    tile = MLP_TILE
    assert tokens % tile == 0 and d == D_MODEL
    x_spec = pl.BlockSpec((tile, d), lambda i: (i, 0))
    operands = (g_pre, w_up, w_down, g_post)
    return pl.pallas_call(
        _mlp_kernel,
        grid=(tokens // tile,),
        in_specs=[x_spec, _resident(g_pre.shape), pl.BlockSpec(memory_space=pl.ANY),
                  pl.BlockSpec(memory_space=pl.ANY), _resident(g_post.shape)],
        out_specs=x_spec,
        out_shape=jax.ShapeDtypeStruct(x.shape, x.dtype),
        scratch_shapes=[
            pltpu.VMEM((d, D_FF + LANES), jnp.bfloat16),
            pltpu.VMEM((D_FF, d + LANES), jnp.bfloat16),
            pltpu.VMEM((2, W_UP_CHUNK_ROWS, w_up.shape[1]), jnp.float32),
            pltpu.VMEM((2, W_DOWN_CHUNK_ROWS, w_down.shape[1]), jnp.float32),
            pltpu.SemaphoreType.DMA((2,)),
        ],
        compiler_params=pltpu.CompilerParams(
            dimension_semantics=("arbitrary",),
            vmem_limit_bytes=VMEM_LIMIT_BYTES),
        name="mlp",
    )(x, *operands)


def kernel(x, norm_mix_pre, w_in, pool_w, pool_scale, w_pool_out, conv_w, conv_b, conv_ln_g,
           conv_ln_b, w_conv_out, w_o, norm_mix_post, norm_mlp_pre, w_up, w_down, norm_mlp_post):
    batch, seq, d = x.shape
    row = lambda p: p.reshape(1, -1)
    for l in range(w_in.shape[0]):
        x = _mixer(x, row(norm_mix_pre[l]), w_in[l], pool_w[l],
                   pool_scale[l].reshape(len(POOL_WINDOWS), 1, POOL_GROUP_DIM), w_pool_out[l],
                   conv_w, l, row(conv_b[l]), row(conv_ln_g[l]), row(conv_ln_b[l]), w_conv_out[l],
                   w_o[l], row(norm_mix_post[l]))
        x = _mlp(x.reshape(batch * seq, d), row(norm_mlp_pre[l]), w_up[l], w_down[l],
                 row(norm_mlp_post[l])).reshape(batch, seq, d)
    return x
```

```python
import functools

import jax
import jax.numpy as jnp
from jax import lax
from jax.experimental import pallas as pl
from jax.experimental.pallas import tpu as pltpu

D_MODEL = 1024
POOL_WIDTH = 512
POOL_GROUP_DIM = 128
POOL_WINDOWS = (2, 4, 8, 16)
CONV_WIDTH = 512
CONV_K = 31
D_FF = 4 * D_MODEL
EPS = 1e-6

SUBLANES = 8
LANES = 128
POOL_HALO = 16
CONV_HALO = 32
CONV_PITCH = CONV_WIDTH + LANES
GLU_OFF = POOL_WIDTH
GATE_A_OFF = POOL_WIDTH + 2 * CONV_WIDTH
_GLU_HALF = CONV_WIDTH // 2
W_IN_COL_STARTS = (
    list(range(0, GLU_OFF, LANES))
    + [GLU_OFF + part * CONV_WIDTH + j * _GLU_HALF + c
       for j in range(2) for part in range(2) for c in range(0, _GLU_HALF, LANES)]
    + list(range(GATE_A_OFF, GATE_A_OFF + 2 * D_MODEL, LANES)))

MIXER_TILE = 512
MLP_TILE = 1024
FF_CHUNK = 1024
PROJ_COLS = 512
CONV_ROWS = 64
PACKED_ROWS = 16
W_IN_CHUNK_ROWS = 128
W_OUT_CHUNK_ROWS = 256
W_UP_CHUNK_ROWS = 128
W_DOWN_CHUNK_ROWS = 512
VMEM_LIMIT_BYTES = 56 * 1024 * 1024


def _dot(a, b):
    return jnp.dot(a, b, preferred_element_type=jnp.float32)


def _rmsnorm(x, g):
    ms = jnp.mean(x * x, axis=-1, keepdims=True)
    return x * lax.rsqrt(ms + EPS) * g


def _sigmoid(x):
    return 0.5 * jnp.tanh(0.5 * x) + 0.5


def _zero_after(v):
    return jnp.minimum(jnp.abs(v), 0.0)


def _repeat(v, rows, cols):
    v = jnp.concatenate([v] * (cols // v.shape[1]), axis=1)
    return jnp.concatenate([v] * (rows // v.shape[0]), axis=0)


def _stream_rows(src_hbm, stage, sem, chunk_rows, consume):
    n_chunks = src_hbm.shape[0] // chunk_rows

    def copy(c):
        return pltpu.make_async_copy(src_hbm.at[pl.ds(c * chunk_rows, chunk_rows), :],
                                     stage.at[c % 2], sem.at[c % 2])

    copy(0).start()
    for c in range(n_chunks):
        if c + 1 < n_chunks:
            copy(c + 1).start()
        copy(c).wait()
        consume(c, stage[c % 2])


def _load_as_bf16(src_hbm, dst_ref, stage, sem, chunk_rows, col_starts=None):
    def cast(c, chunk):
        if col_starts is not None:
            chunk = jnp.concatenate([chunk[:, c0:c0 + LANES] for c0 in col_starts], axis=1)
        dst_ref[c * chunk_rows:(c + 1) * chunk_rows, 0:chunk.shape[1]] = chunk.astype(jnp.bfloat16)

    _stream_rows(src_hbm, stage, sem, chunk_rows, cast)


def _load_folded_pool(w_pool_out_hbm, pool_w_ref, pool_scale_ref, dst_ref, stage, sem, chunk_rows):
    groups_per_chunk = chunk_rows // POOL_GROUP_DIM

    def fold(c, chunk):
        for j in range(groups_per_chunk):
            g = c * groups_per_chunk + j
            scaled = pool_w_ref[g] * pool_scale_ref[g]
            rows = chunk[j * POOL_GROUP_DIM:(j + 1) * POOL_GROUP_DIM, :]
            folded = jnp.dot(scaled, rows, precision=lax.Precision.HIGHEST,
                             preferred_element_type=jnp.float32)
            dst_ref[g * POOL_GROUP_DIM:(g + 1) * POOL_GROUP_DIM, :] = folded.astype(jnp.bfloat16)

    _stream_rows(w_pool_out_hbm, stage, sem, chunk_rows, fold)


def _mixer_kernel(x_ref, g_pre_ref, w_in_hbm, pool_w_ref, pool_scale_ref, w_pool_out_hbm,
                  conv_w_ref, conv_b_ref, ln_g_ref, ln_b_ref, w_conv_out_hbm, w_o_hbm,
                  g_post_ref, o_ref,
                  w_in_ref, w_pool_ref, w_conv_out_ref, w_o_ref, stage_in, stage_out, sem,
                  h_ref, pool_ext, pooled_ref, glu_ref, conv_x, conv_out, y_ref, gate_ref, merged_ref,
                  *, tile):
    t = pl.program_id(1)
    conv_rows = CONV_HALO + tile

    @pl.when((pl.program_id(0) == 0) & (t == 0))
    def _():
        _load_as_bf16(w_in_hbm, w_in_ref, stage_in, sem, W_IN_CHUNK_ROWS, col_starts=W_IN_COL_STARTS)
        _load_folded_pool(w_pool_out_hbm, pool_w_ref, pool_scale_ref, w_pool_ref, stage_out, sem,
                          W_OUT_CHUNK_ROWS)
        _load_as_bf16(w_conv_out_hbm, w_conv_out_ref, stage_out, sem, W_OUT_CHUNK_ROWS)
        _load_as_bf16(w_o_hbm, w_o_ref, stage_out, sem, W_OUT_CHUNK_ROWS)

    @pl.when(t == 0)
    def _():
        pool_ext[0:POOL_HALO, :] = jnp.zeros((POOL_HALO, POOL_WIDTH), jnp.float32)
        conv_x[0, 0:CONV_HALO, 0:CONV_WIDTH] = jnp.zeros((CONV_HALO, CONV_WIDTH), jnp.float32)

    def project(c0):
        return _dot(h_ref[...], w_in_ref[:, c0:c0 + PROJ_COLS])

    h_ref[...] = _rmsnorm(x_ref[...], g_pre_ref[...]).astype(jnp.bfloat16)

    half = CONV_WIDTH // 2
    for j in range(2):
        glu_ref[:, j * PROJ_COLS:(j + 1) * PROJ_COLS] = project(GLU_OFF + j * PROJ_COLS)
        for r0 in range(0, tile, CONV_ROWS):
            rows = slice(r0, r0 + CONV_ROWS)
            conv_x[0, CONV_HALO + r0:CONV_HALO + r0 + CONV_ROWS, j * half:(j + 1) * half] = (
                glu_ref[rows, j * PROJ_COLS:j * PROJ_COLS + half]
                * _sigmoid(glu_ref[rows, j * PROJ_COLS + half:(j + 1) * PROJ_COLS]))
        if j == 0:
            pool_ext[POOL_HALO:POOL_HALO + tile, :] = project(0)
        for c0 in range(j * half, (j + 1) * half, LANES):
            cols = slice(c0, c0 + LANES)
            x0 = conv_x[0, :, cols]
            for s in range(1, SUBLANES):
                conv_x[s, :, cols] = pltpu.roll(x0, conv_rows - s, axis=0)

    frame = t * tile + lax.broadcasted_iota(jnp.int32, (tile, 1), 0)
    for g, window in enumerate(POOL_WINDOWS):
        cols = slice(g * POOL_GROUP_DIM, (g + 1) * POOL_GROUP_DIM)
        ext = pool_ext[:, cols]
        s = ext
        step = 1
        while step < window:
            s = s + pltpu.roll(s, step, axis=0)
            step *= 2
        inv_count = 1.0 / jnp.minimum(frame + 1, window).astype(jnp.float32)
        pooled = s[POOL_HALO:, :] * inv_count - ext[POOL_HALO:, :]
        pooled_ref[:, cols] = pooled.astype(jnp.bfloat16)

    first_tap = CONV_HALO - (CONV_K - 1)
    for r0 in range(0, tile, CONV_ROWS):
        accs = []
        for c0 in range(0, CONV_WIDTH, LANES):
            cols = slice(c0, c0 + LANES)
            acc = jnp.broadcast_to(conv_b_ref[:, cols], (CONV_ROWS, LANES))
            for k in range(CONV_K):
                q, s = divmod(first_tap + k, SUBLANES)
                lo = r0 + q * SUBLANES
                acc = acc + conv_w_ref[k:k + 1, cols] * conv_x[s, lo:lo + CONV_ROWS, cols]
            accs.append(acc)
        conv_out[r0:r0 + CONV_ROWS, :] = jnp.concatenate(accs, axis=-1)

    def gate_piece(i, after):
        lhs = h_ref[...]
        if after is not None:
            lhs = lhs + _repeat(_zero_after(after).astype(jnp.bfloat16), tile, D_MODEL)
        c0 = i * PROJ_COLS
        piece = _sigmoid(_dot(lhs, w_in_ref[:, GATE_A_OFF + c0:GATE_A_OFF + c0 + PROJ_COLS]))
        gate_ref[:, c0:c0 + PROJ_COLS] = piece
        return piece[tile - PACKED_ROWS:, PROJ_COLS - LANES:]

    def norm_stage(i, after):
        for r0 in range(i * stage_rows, (i + 1) * stage_rows, CONV_ROWS):
            c = conv_out[r0:r0 + CONV_ROWS, :]
            if after is not None:
                c = c + _repeat(_zero_after(after), CONV_ROWS, CONV_WIDTH)
            mu = jnp.mean(c, axis=-1, keepdims=True)
            cc = c - mu
            var = jnp.mean(cc * cc, axis=-1, keepdims=True)
            y = cc * lax.rsqrt(var + EPS) * ln_g_ref[...] + ln_b_ref[...]
            y = y * _sigmoid(y)
            y_ref[r0:r0 + CONV_ROWS, :] = y.astype(jnp.bfloat16)
        return y[CONV_ROWS - PACKED_ROWS:, CONV_WIDTH - LANES:]

    n_pieces = 2 * D_MODEL // PROJ_COLS
    stage_rows = tile // n_pieces
    mxu_done = vpu_done = None
    for i in range(n_pieces):
        piece_done = gate_piece(i, vpu_done)
        vpu_done = norm_stage(i, mxu_done)
        mxu_done = piece_done

    for c0 in range(0, D_MODEL, PROJ_COLS):
        cols = slice(c0, c0 + PROJ_COLS)
        a = _dot(pooled_ref[...], w_pool_ref[:, cols])
        b = _dot(y_ref[...], w_conv_out_ref[:, cols])
        merged = (gate_ref[:, cols] * a
                  + gate_ref[:, D_MODEL + c0:D_MODEL + c0 + PROJ_COLS] * b)
        merged_ref[:, cols] = merged.astype(jnp.bfloat16)
    mix = _dot(merged_ref[...], w_o_ref[...])
    o_ref[...] = x_ref[...] + _rmsnorm(mix, g_post_ref[...])

    pool_ext[0:POOL_HALO, :] = pool_ext[tile:tile + POOL_HALO, :]
    conv_x[0, 0:CONV_HALO, 0:CONV_WIDTH] = conv_x[0, tile:tile + CONV_HALO, 0:CONV_WIDTH]


def _mlp_kernel(x_ref, g_pre_ref, w_up_hbm, w_down_hbm, g_post_ref, o_ref,
                w_up_ref, w_down_ref, stage_up, stage_down, sem):
    @pl.when(pl.program_id(0) == 0)
    def _():
        _load_as_bf16(w_up_hbm, w_up_ref, stage_up, sem, W_UP_CHUNK_ROWS)
        _load_as_bf16(w_down_hbm, w_down_ref, stage_down, sem, W_DOWN_CHUNK_ROWS)

    x = x_ref[...]
    h = _rmsnorm(x, g_pre_ref[...]).astype(jnp.bfloat16)
    acc = None
    for c0 in range(0, D_FF, FF_CHUNK):
        a = jnp.maximum(_dot(h, w_up_ref[:, c0:c0 + FF_CHUNK]), 0.0)
        p = _dot((a * a).astype(jnp.bfloat16), w_down_ref[c0:c0 + FF_CHUNK, 0:D_MODEL])
        acc = p if acc is None else acc + p
    o_ref[...] = x + _rmsnorm(acc, g_post_ref[...])


def _resident(shape):
    return pl.BlockSpec(shape, lambda *_: (0,) * len(shape), pipeline_mode=pl.Buffered(1))


def _mixer(x, g_pre, w_in, pool_w, pool_scale, w_pool_out, conv_w, layer, conv_b, ln_g, ln_b,
           w_conv_out, w_o, g_post):
    batch, seq, d = x.shape
    tile = MIXER_TILE
    assert seq % tile == 0 and d == D_MODEL
    x_spec = pl.BlockSpec((None, tile, d), lambda b, t: (b, t, 0))
    operands = (g_pre, w_in, pool_w, pool_scale, w_pool_out, conv_w, conv_b, ln_g, ln_b,
                w_conv_out, w_o, g_post)
    in_hbm = (w_in, w_pool_out, w_conv_out, w_o)
    hbm = pl.BlockSpec(memory_space=pl.ANY)
    conv_w_spec = pl.BlockSpec((None,) + conv_w.shape[1:], lambda *_: (layer, 0, 0),
                               pipeline_mode=pl.Buffered(1))

    def spec(a):
        if a is conv_w:
            return conv_w_spec
        return hbm if any(a is w for w in in_hbm) else _resident(a.shape)

    return pl.pallas_call(
        functools.partial(_mixer_kernel, tile=tile),
        grid=(batch, seq // tile),
        in_specs=[x_spec] + [spec(a) for a in operands],
        out_specs=x_spec,
        out_shape=jax.ShapeDtypeStruct(x.shape, x.dtype),
        scratch_shapes=[
            pltpu.VMEM(w_in.shape, jnp.bfloat16),
            pltpu.VMEM(w_pool_out.shape, jnp.bfloat16),
            pltpu.VMEM(w_conv_out.shape, jnp.bfloat16),
            pltpu.VMEM(w_o.shape, jnp.bfloat16),
            pltpu.VMEM((2, W_IN_CHUNK_ROWS, w_in.shape[1]), jnp.float32),
            pltpu.VMEM((2, W_OUT_CHUNK_ROWS, d), jnp.float32),
            pltpu.SemaphoreType.DMA((2,)),
            pltpu.VMEM((tile, d), jnp.bfloat16),
            pltpu.VMEM((POOL_HALO + tile, POOL_WIDTH), jnp.float32),
            pltpu.VMEM((tile, POOL_WIDTH), jnp.bfloat16),
            pltpu.VMEM((tile, 2 * CONV_WIDTH), jnp.float32),
            pltpu.VMEM((SUBLANES, CONV_HALO + tile, CONV_PITCH), jnp.float32),
            pltpu.VMEM((tile, CONV_WIDTH), jnp.float32),
            pltpu.VMEM((tile, CONV_WIDTH), jnp.bfloat16),
            pltpu.VMEM((tile, 2 * d), jnp.float32),
            pltpu.VMEM((tile, d), jnp.bfloat16),
        ],
        compiler_params=pltpu.CompilerParams(
            dimension_semantics=("arbitrary", "arbitrary"),
            vmem_limit_bytes=VMEM_LIMIT_BYTES),
        name="mixer",
    )(x, *operands)


def _mlp(x, g_pre, w_up, w_down, g_post):
    tokens, d = x.shape
    tile = MLP_TILE
    assert tokens % tile == 0 and d == D_MODEL
    x_spec = pl.BlockSpec((tile, d), lambda i: (i, 0))
    operands = (g_pre, w_up, w_down, g_post)
    return pl.pallas_call(
        _mlp_kernel,
        grid=(tokens // tile,),
        in_specs=[x_spec, _resident(g_pre.shape), pl.BlockSpec(memory_space=pl.ANY),
                  pl.BlockSpec(memory_space=pl.ANY), _resident(g_post.shape)],
        out_specs=x_spec,
        out_shape=jax.ShapeDtypeStruct(x.shape, x.dtype),
        scratch_shapes=[
            pltpu.VMEM((d, D_FF + LANES), jnp.bfloat16),
            pltpu.VMEM((D_FF, d + LANES), jnp.bfloat16),
            pltpu.VMEM((2, W_UP_CHUNK_ROWS, w_up.shape[1]), jnp.float32),
            pltpu.VMEM((2, W_DOWN_CHUNK_ROWS, w_down.shape[1]), jnp.float32),
            pltpu.SemaphoreType.DMA((2,)),
        ],
        compiler_params=pltpu.CompilerParams(
            dimension_semantics=("arbitrary",),
            vmem_limit_bytes=VMEM_LIMIT_BYTES),
        name="mlp",
    )(x, *operands)


def kernel(x, norm_mix_pre, w_in, pool_w, pool_scale, w_pool_out, conv_w, conv_b, conv_ln_g,
           conv_ln_b, w_conv_out, w_o, norm_mix_post, norm_mlp_pre, w_up, w_down, norm_mlp_post):
    batch, seq, d = x.shape
    row = lambda p: p.reshape(1, -1)
    for l in range(w_in.shape[0]):
        x = _mixer(x, row(norm_mix_pre[l]), w_in[l], pool_w[l],
                   pool_scale[l].reshape(len(POOL_WINDOWS), 1, POOL_GROUP_DIM), w_pool_out[l],
                   conv_w, l, row(conv_b[l]), row(conv_ln_g[l]), row(conv_ln_b[l]), w_conv_out[l],
                   w_o[l], row(norm_mix_post[l]))
        x = _mlp(x.reshape(batch * seq, d), row(norm_mlp_pre[l]), w_up[l], w_down[l],
                 row(norm_mlp_post[l])).reshape(batch, seq, d)
    return x
```

```python
import functools

import jax
import jax.numpy as jnp
from jax import lax
from jax.experimental import pallas as pl
from jax.experimental.pallas import tpu as pltpu

D_MODEL = 1024
POOL_WIDTH = 512
POOL_GROUP_DIM = 128
POOL_WINDOWS = (2, 4, 8, 16)
CONV_WIDTH = 512
CONV_K = 31
D_FF = 4 * D_MODEL
EPS = 1e-6

SUBLANES = 8
LANES = 128
POOL_HALO = 16
CONV_HALO = 32
CONV_PITCH = CONV_WIDTH + LANES
GLU_OFF = POOL_WIDTH
GATE_A_OFF = POOL_WIDTH + 2 * CONV_WIDTH
_GLU_HALF = CONV_WIDTH // 2
W_IN_COL_STARTS = (
    list(range(0, GLU_OFF, LANES))
    + [GLU_OFF + part * CONV_WIDTH + j * _GLU_HALF + c
       for j in range(2) for part in range(2) for c in range(0, _GLU_HALF, LANES)]
    + list(range(GATE_A_OFF, GATE_A_OFF + 2 * D_MODEL, LANES)))

MIXER_TILE = 512
MLP_TILE = 1024
FF_CHUNK = 1024
PROJ_COLS = 512
CONV_ROWS = 64
PACKED_ROWS = 16
W_IN_CHUNK_ROWS = 128
W_OUT_CHUNK_ROWS = 256
W_UP_CHUNK_ROWS = 128
W_DOWN_CHUNK_ROWS = 512
VMEM_LIMIT_BYTES = 56 * 1024 * 1024


def _dot(a, b):
    return jnp.dot(a, b, preferred_element_type=jnp.float32)


def _rmsnorm(x, g):
    ms = jnp.mean(x * x, axis=-1, keepdims=True)
    return x * lax.rsqrt(ms + EPS) * g


def _sigmoid(x):
    return 0.5 * jnp.tanh(0.5 * x) + 0.5


def _zero_after(v):
    return jnp.minimum(jnp.abs(v), 0.0)


def _repeat(v, rows, cols):
    v = jnp.concatenate([v] * (cols // v.shape[1]), axis=1)
    return jnp.concatenate([v] * (rows // v.shape[0]), axis=0)


def _stream_rows(src_hbm, stage, sem, chunk_rows, consume):
    n_chunks = src_hbm.shape[0] // chunk_rows

    def copy(c):
        return pltpu.make_async_copy(src_hbm.at[pl.ds(c * chunk_rows, chunk_rows), :],
                                     stage.at[c % 2], sem.at[c % 2])

    copy(0).start()
    for c in range(n_chunks):
        if c + 1 < n_chunks:
            copy(c + 1).start()
        copy(c).wait()
        consume(c, stage[c % 2])


def _load_as_bf16(src_hbm, dst_ref, stage, sem, chunk_rows, col_starts=None):
    def cast(c, chunk):
        if col_starts is not None:
            chunk = jnp.concatenate([chunk[:, c0:c0 + LANES] for c0 in col_starts], axis=1)
        dst_ref[c * chunk_rows:(c + 1) * chunk_rows, 0:chunk.shape[1]] = chunk.astype(jnp.bfloat16)

    _stream_rows(src_hbm, stage, sem, chunk_rows, cast)


def _load_folded_pool(w_pool_out_hbm, pool_w_ref, pool_scale_ref, dst_ref, stage, sem, chunk_rows):
    groups_per_chunk = chunk_rows // POOL_GROUP_DIM

    def fold(c, chunk):
        for j in range(groups_per_chunk):
            g = c * groups_per_chunk + j
            scaled = pool_w_ref[g] * pool_scale_ref[g]
            rows = chunk[j * POOL_GROUP_DIM:(j + 1) * POOL_GROUP_DIM, :]
            folded = jnp.dot(scaled, rows, precision=lax.Precision.HIGHEST,
                             preferred_element_type=jnp.float32)
            dst_ref[g * POOL_GROUP_DIM:(g + 1) * POOL_GROUP_DIM, :] = folded.astype(jnp.bfloat16)

    _stream_rows(w_pool_out_hbm, stage, sem, chunk_rows, fold)


def _mixer_kernel(x_ref, g_pre_ref, w_in_hbm, pool_w_ref, pool_scale_ref, w_pool_out_hbm,
                  conv_w_ref, conv_b_ref, ln_g_ref, ln_b_ref, w_conv_out_hbm, w_o_hbm,
                  g_post_ref, o_ref,
                  w_in_ref, w_pool_ref, w_conv_out_ref, w_o_ref, stage_in, stage_out, sem,
                  h_ref, pool_ext, pooled_ref, glu_ref, conv_x, conv_out, y_ref, gate_ref, merged_ref,
                  *, tile):
    t = pl.program_id(1)
    conv_rows = CONV_HALO + tile

    @pl.when((pl.program_id(0) == 0) & (t == 0))
    def _():
        _load_as_bf16(w_in_hbm, w_in_ref, stage_in, sem, W_IN_CHUNK_ROWS, col_starts=W_IN_COL_STARTS)
        _load_folded_pool(w_pool_out_hbm, pool_w_ref, pool_scale_ref, w_pool_ref, stage_out, sem,
                          W_OUT_CHUNK_ROWS)
        _load_as_bf16(w_conv_out_hbm, w_conv_out_ref, stage_out, sem, W_OUT_CHUNK_ROWS)
        _load_as_bf16(w_o_hbm, w_o_ref, stage_out, sem, W_OUT_CHUNK_ROWS)

    @pl.when(t == 0)
    def _():
        pool_ext[0:POOL_HALO, :] = jnp.zeros((POOL_HALO, POOL_WIDTH), jnp.float32)
        conv_x[0, 0:CONV_HALO, 0:CONV_WIDTH] = jnp.zeros((CONV_HALO, CONV_WIDTH), jnp.float32)

    def project(c0):
        return _dot(h_ref[...], w_in_ref[:, c0:c0 + PROJ_COLS])

    h_ref[...] = _rmsnorm(x_ref[...], g_pre_ref[...]).astype(jnp.bfloat16)

    half = CONV_WIDTH // 2
    for j in range(2):
        glu_ref[:, j * PROJ_COLS:(j + 1) * PROJ_COLS] = project(GLU_OFF + j * PROJ_COLS)
        for r0 in range(0, tile, CONV_ROWS):
            rows = slice(r0, r0 + CONV_ROWS)
            conv_x[0, CONV_HALO + r0:CONV_HALO + r0 + CONV_ROWS, j * half:(j + 1) * half] = (
                glu_ref[rows, j * PROJ_COLS:j * PROJ_COLS + half]
                * _sigmoid(glu_ref[rows, j * PROJ_COLS + half:(j + 1) * PROJ_COLS]))
        if j == 0:
            pool_ext[POOL_HALO:POOL_HALO + tile, :] = project(0)
        for c0 in range(j * half, (j + 1) * half, LANES):
            cols = slice(c0, c0 + LANES)
            x0 = conv_x[0, :, cols]
            for s in range(1, SUBLANES):
                conv_x[s, :, cols] = pltpu.roll(x0, conv_rows - s, axis=0)

    head_frame = t * tile + lax.broadcasted_iota(jnp.int32, (POOL_HALO, 1), 0)
    head_rows = slice(POOL_HALO, 2 * POOL_HALO)
    for g, window in enumerate(POOL_WINDOWS):
        cols = slice(g * POOL_GROUP_DIM, (g + 1) * POOL_GROUP_DIM)
        ext = pool_ext[:, cols]
        s = ext
        step = 1
        while step < window:
            s = s + pltpu.roll(s, step, axis=0)
            step *= 2
        inv_count = 1.0 / jnp.minimum(head_frame + 1, window).astype(jnp.float32)
        head = s[head_rows, :] * inv_count - ext[head_rows, :]
        body = s[2 * POOL_HALO:, :] * (1.0 / window) - ext[2 * POOL_HALO:, :]
        pooled_ref[0:POOL_HALO, cols] = head.astype(jnp.bfloat16)
        pooled_ref[POOL_HALO:, cols] = body.astype(jnp.bfloat16)

    first_tap = CONV_HALO - (CONV_K - 1)
    for r0 in range(0, tile, CONV_ROWS):
        accs = []
        for c0 in range(0, CONV_WIDTH, LANES):
            cols = slice(c0, c0 + LANES)
            acc = jnp.broadcast_to(conv_b_ref[:, cols], (CONV_ROWS, LANES))
            for k in range(CONV_K):
                q, s = divmod(first_tap + k, SUBLANES)
                lo = r0 + q * SUBLANES
                acc = acc + conv_w_ref[k:k + 1, cols] * conv_x[s, lo:lo + CONV_ROWS, cols]
            accs.append(acc)
        conv_out[r0:r0 + CONV_ROWS, :] = jnp.concatenate(accs, axis=-1)

    def gate_piece(i, after):
        lhs = h_ref[...]
        if after is not None:
            lhs = lhs + _repeat(_zero_after(after).astype(jnp.bfloat16), tile, D_MODEL)
        c0 = i * PROJ_COLS
        piece = _sigmoid(_dot(lhs, w_in_ref[:, GATE_A_OFF + c0:GATE_A_OFF + c0 + PROJ_COLS]))
        gate_ref[:, c0:c0 + PROJ_COLS] = piece
        return piece[tile - PACKED_ROWS:, PROJ_COLS - LANES:]

    def norm_stage(i, after):
        for r0 in range(i * stage_rows, (i + 1) * stage_rows, CONV_ROWS):
            c = conv_out[r0:r0 + CONV_ROWS, :]
            if after is not None:
                c = c + _repeat(_zero_after(after), CONV_ROWS, CONV_WIDTH)
            mu = jnp.mean(c, axis=-1, keepdims=True)
            cc = c - mu
            var = jnp.mean(cc * cc, axis=-1, keepdims=True)
            y = cc * lax.rsqrt(var + EPS) * ln_g_ref[...] + ln_b_ref[...]
            y = y * _sigmoid(y)
            y_ref[r0:r0 + CONV_ROWS, :] = y.astype(jnp.bfloat16)
        return y[CONV_ROWS - PACKED_ROWS:, CONV_WIDTH - LANES:]

    n_pieces = 2 * D_MODEL // PROJ_COLS
    stage_rows = tile // n_pieces
    mxu_done = vpu_done = None
    for i in range(n_pieces):
        piece_done = gate_piece(i, vpu_done)
        vpu_done = norm_stage(i, mxu_done)
        mxu_done = piece_done

    for c0 in range(0, D_MODEL, PROJ_COLS):
        cols = slice(c0, c0 + PROJ_COLS)
        a = _dot(pooled_ref[...], w_pool_ref[:, cols])
        b = _dot(y_ref[...], w_conv_out_ref[:, cols])
        merged = (gate_ref[:, cols] * a
                  + gate_ref[:, D_MODEL + c0:D_MODEL + c0 + PROJ_COLS] * b)
        merged_ref[:, cols] = merged.astype(jnp.bfloat16)
    mix = _dot(merged_ref[...], w_o_ref[...])
    o_ref[...] = x_ref[...] + _rmsnorm(mix, g_post_ref[...])

    pool_ext[0:POOL_HALO, :] = pool_ext[tile:tile + POOL_HALO, :]
    conv_x[0, 0:CONV_HALO, 0:CONV_WIDTH] = conv_x[0, tile:tile + CONV_HALO, 0:CONV_WIDTH]


def _mlp_kernel(x_ref, g_pre_ref, w_up_hbm, w_down_hbm, g_post_ref, o_ref,
                w_up_ref, w_down_ref, stage_up, stage_down, sem):
    @pl.when(pl.program_id(0) == 0)
    def _():
        _load_as_bf16(w_up_hbm, w_up_ref, stage_up, sem, W_UP_CHUNK_ROWS)
        _load_as_bf16(w_down_hbm, w_down_ref, stage_down, sem, W_DOWN_CHUNK_ROWS)

    x = x_ref[...]
    h = _rmsnorm(x, g_pre_ref[...]).astype(jnp.bfloat16)
    acc = None
    for c0 in range(0, D_FF, FF_CHUNK):
        a = jnp.maximum(_dot(h, w_up_ref[:, c0:c0 + FF_CHUNK]), 0.0)
        p = _dot((a * a).astype(jnp.bfloat16), w_down_ref[c0:c0 + FF_CHUNK, 0:D_MODEL])
        acc = p if acc is None else acc + p
    o_ref[...] = x + _rmsnorm(acc, g_post_ref[...])


def _resident(shape):
    return pl.BlockSpec(shape, lambda *_: (0,) * len(shape), pipeline_mode=pl.Buffered(1))


def _mixer(x, g_pre, w_in, pool_w, pool_scale, w_pool_out, conv_w, layer, conv_b, ln_g, ln_b,
           w_conv_out, w_o, g_post):
    batch, seq, d = x.shape
    tile = MIXER_TILE
    assert seq % tile == 0 and d == D_MODEL
    x_spec = pl.BlockSpec((None, tile, d), lambda b, t: (b, t, 0))
    operands = (g_pre, w_in, pool_w, pool_scale, w_pool_out, conv_w, conv_b, ln_g, ln_b,
                w_conv_out, w_o, g_post)
    in_hbm = (w_in, w_pool_out, w_conv_out, w_o)
    hbm = pl.BlockSpec(memory_space=pl.ANY)
    conv_w_spec = pl.BlockSpec((None,) + conv_w.shape[1:], lambda *_: (layer, 0, 0),
                               pipeline_mode=pl.Buffered(1))

    def spec(a):
        if a is conv_w:
            return conv_w_spec
        return hbm if any(a is w for w in in_hbm) else _resident(a.shape)

    return pl.pallas_call(
        functools.partial(_mixer_kernel, tile=tile),
        grid=(batch, seq // tile),
        in_specs=[x_spec] + [spec(a) for a in operands],
        out_specs=x_spec,
        out_shape=jax.ShapeDtypeStruct(x.shape, x.dtype),
        scratch_shapes=[
            pltpu.VMEM(w_in.shape, jnp.bfloat16),
            pltpu.VMEM(w_pool_out.shape, jnp.bfloat16),
            pltpu.VMEM(w_conv_out.shape, jnp.bfloat16),
            pltpu.VMEM(w_o.shape, jnp.bfloat16),
            pltpu.VMEM((2, W_IN_CHUNK_ROWS, w_in.shape[1]), jnp.float32),
            pltpu.VMEM((2, W_OUT_CHUNK_ROWS, d), jnp.float32),
            pltpu.SemaphoreType.DMA((2,)),
            pltpu.VMEM((tile, d), jnp.bfloat16),
            pltpu.VMEM((POOL_HALO + tile, POOL_WIDTH), jnp.float32),
            pltpu.VMEM((tile, POOL_WIDTH), jnp.bfloat16),
            pltpu.VMEM((tile, 2 * CONV_WIDTH), jnp.float32),
            pltpu.VMEM((SUBLANES, CONV_HALO + tile, CONV_PITCH), jnp.float32),
            pltpu.VMEM((tile, CONV_WIDTH), jnp.float32),
            pltpu.VMEM((tile, CONV_WIDTH), jnp.bfloat16),
            pltpu.VMEM((tile, 2 * d), jnp.float32),
            pltpu.VMEM((tile, d), jnp.bfloat16),
        ],
        compiler_params=pltpu.CompilerParams(
            dimension_semantics=("arbitrary", "arbitrary"),
            vmem_limit_bytes=VMEM_LIMIT_BYTES),
        name="mixer",
    )(x, *operands)


def _mlp(x, g_pre, w_up, w_down, g_post):
    tokens, d = x.shape
    tile = MLP_TILE
    assert tokens % tile == 0 and d == D_MODEL
    x_spec = pl.BlockSpec((tile, d), lambda i: (i, 0))
    operands = (g_pre, w_up, w_down, g_post)
    return pl.pallas_call(
        _mlp_kernel,
        grid=(tokens // tile,),
        in_specs=[x_spec, _resident(g_pre.shape), pl.BlockSpec(memory_space=pl.ANY),
                  pl.BlockSpec(memory_space=pl.ANY), _resident(g_post.shape)],
        out_specs=x_spec,
        out_shape=jax.ShapeDtypeStruct(x.shape, x.dtype),
        scratch_shapes=[
            pltpu.VMEM((d, D_FF + LANES), jnp.bfloat16),
            pltpu.VMEM((D_FF, d + LANES), jnp.bfloat16),
            pltpu.VMEM((2, W_UP_CHUNK_ROWS, w_up.shape[1]), jnp.float32),
            pltpu.VMEM((2, W_DOWN_CHUNK_ROWS, w_down.shape[1]), jnp.float32),
            pltpu.SemaphoreType.DMA((2,)),
        ],
        compiler_params=pltpu.CompilerParams(
            dimension_semantics=("arbitrary",),
            vmem_limit_bytes=VMEM_LIMIT_BYTES),
        name="mlp",
    )(x, *operands)


def kernel(x, norm_mix_pre, w_in, pool_w, pool_scale, w_pool_out, conv_w, conv_b, conv_ln_g,
           conv_ln_b, w_conv_out, w_o, norm_mix_post, norm_mlp_pre, w_up, w_down, norm_mlp_post):
    batch, seq, d = x.shape
    row = lambda p: p.reshape(1, -1)
    for l in range(w_in.shape[0]):
        x = _mixer(x, row(norm_mix_pre[l]), w_in[l], pool_w[l],
                   pool_scale[l].reshape(len(POOL_WINDOWS), 1, POOL_GROUP_DIM), w_pool_out[l],
                   conv_w, l, row(conv_b[l]), row(conv_ln_g[l]), row(conv_ln_b[l]), w_conv_out[l],
                   w_o[l], row(norm_mix_post[l]))
        x = _mlp(x.reshape(batch * seq, d), row(norm_mlp_pre[l]), w_up[l], w_down[l],
                 row(norm_mlp_post[l])).reshape(batch, seq, d)
    return x
```

```python
import functools

import jax
import jax.numpy as jnp
from jax import lax
from jax.experimental import pallas as pl
from jax.experimental.pallas import tpu as pltpu

D_MODEL = 1024
POOL_WIDTH = 512
POOL_GROUP_DIM = 128
POOL_WINDOWS = (2, 4, 8, 16)
CONV_WIDTH = 512
CONV_K = 31
D_FF = 4 * D_MODEL
EPS = 1e-6

SUBLANES = 8
LANES = 128
POOL_HALO = 16
CONV_HALO = 32
CONV_PITCH = CONV_WIDTH + LANES
GLU_OFF = POOL_WIDTH
GATE_A_OFF = POOL_WIDTH + 2 * CONV_WIDTH
_GLU_HALF = CONV_WIDTH // 2
W_IN_COL_STARTS = (
    list(range(0, GLU_OFF, LANES))
    + [GLU_OFF + part * CONV_WIDTH + j * _GLU_HALF + c
       for j in range(2) for part in range(2) for c in range(0, _GLU_HALF, LANES)]
    + list(range(GATE_A_OFF, GATE_A_OFF + 2 * D_MODEL, LANES)))
W_IN_COL_SCALES = (
    [1.0] * (GLU_OFF // LANES)
    + [(1.0, 0.5)[part] for j in range(2) for part in range(2) for c in range(0, _GLU_HALF, LANES)]
    + [0.5] * (2 * D_MODEL // LANES))

MIXER_TILE = 512
MLP_TILE = 1024
FF_CHUNK = 1024
PROJ_COLS = 512
CONV_ROWS = 64
PACKED_ROWS = 16
W_IN_CHUNK_ROWS = 128
W_OUT_CHUNK_ROWS = 256
W_UP_CHUNK_ROWS = 128
W_DOWN_CHUNK_ROWS = 512
VMEM_LIMIT_BYTES = 56 * 1024 * 1024


def _dot(a, b):
    return jnp.dot(a, b, preferred_element_type=jnp.float32)


def _rmsnorm(x, g):
    ms = jnp.mean(x * x, axis=-1, keepdims=True)
    return x * lax.rsqrt(ms + EPS) * g


def _sigmoid_of_half(half_x):
    return 0.5 * jnp.tanh(half_x) + 0.5


def _zero_after(v):
    return jnp.minimum(jnp.abs(v), 0.0)


def _repeat(v, rows, cols):
    v = jnp.concatenate([v] * (cols // v.shape[1]), axis=1)
    return jnp.concatenate([v] * (rows // v.shape[0]), axis=0)


def _stream_rows(src_hbm, stage, sem, chunk_rows, consume):
    n_chunks = src_hbm.shape[0] // chunk_rows

    def copy(c):
        return pltpu.make_async_copy(src_hbm.at[pl.ds(c * chunk_rows, chunk_rows), :],
                                     stage.at[c % 2], sem.at[c % 2])

    copy(0).start()
    for c in range(n_chunks):
        if c + 1 < n_chunks:
            copy(c + 1).start()
        copy(c).wait()
        consume(c, stage[c % 2])


def _load_as_bf16(src_hbm, dst_ref, stage, sem, chunk_rows, col_starts=None, col_scales=None):
    def cast(c, chunk):
        if col_starts is not None:
            blocks = [chunk[:, c0:c0 + LANES] for c0 in col_starts]
            if col_scales is not None:
                blocks = [b if s == 1.0 else b * s for b, s in zip(blocks, col_scales)]
            chunk = jnp.concatenate(blocks, axis=1)
        dst_ref[c * chunk_rows:(c + 1) * chunk_rows, 0:chunk.shape[1]] = chunk.astype(jnp.bfloat16)

    _stream_rows(src_hbm, stage, sem, chunk_rows, cast)


def _load_folded_pool(w_pool_out_hbm, pool_w_ref, pool_scale_ref, dst_ref, stage, sem, chunk_rows):
    groups_per_chunk = chunk_rows // POOL_GROUP_DIM

    def fold(c, chunk):
        for j in range(groups_per_chunk):
            g = c * groups_per_chunk + j
            scaled = pool_w_ref[g] * pool_scale_ref[g]
            rows = chunk[j * POOL_GROUP_DIM:(j + 1) * POOL_GROUP_DIM, :]
            folded = jnp.dot(scaled, rows, precision=lax.Precision.HIGHEST,
                             preferred_element_type=jnp.float32)
            dst_ref[g * POOL_GROUP_DIM:(g + 1) * POOL_GROUP_DIM, :] = folded.astype(jnp.bfloat16)

    _stream_rows(w_pool_out_hbm, stage, sem, chunk_rows, fold)


def _mixer_kernel(x_ref, g_pre_ref, w_in_hbm, pool_w_ref, pool_scale_ref, w_pool_out_hbm,
                  conv_w_ref, conv_b_ref, ln_g_ref, ln_b_ref, w_conv_out_hbm, w_o_hbm,
                  g_post_ref, o_ref,
                  w_in_ref, w_pool_ref, w_conv_out_ref, w_o_ref, stage_in, stage_out, sem,
                  h_ref, pool_ext, pooled_ref, glu_ref, conv_x, conv_out, y_ref, gate_ref, merged_ref,
                  *, tile):
    t = pl.program_id(1)
    conv_rows = CONV_HALO + tile

    @pl.when((pl.program_id(0) == 0) & (t == 0))
    def _():
        _load_as_bf16(w_in_hbm, w_in_ref, stage_in, sem, W_IN_CHUNK_ROWS,
                      col_starts=W_IN_COL_STARTS, col_scales=W_IN_COL_SCALES)
        _load_folded_pool(w_pool_out_hbm, pool_w_ref, pool_scale_ref, w_pool_ref, stage_out, sem,
                          W_OUT_CHUNK_ROWS)
        _load_as_bf16(w_conv_out_hbm, w_conv_out_ref, stage_out, sem, W_OUT_CHUNK_ROWS)
        _load_as_bf16(w_o_hbm, w_o_ref, stage_out, sem, W_OUT_CHUNK_ROWS)

    @pl.when(t == 0)
    def _():
        pool_ext[0:POOL_HALO, :] = jnp.zeros((POOL_HALO, POOL_WIDTH), jnp.float32)
        conv_x[0, 0:CONV_HALO, 0:CONV_WIDTH] = jnp.zeros((CONV_HALO, CONV_WIDTH), jnp.float32)

    def project(c0):
        return _dot(h_ref[...], w_in_ref[:, c0:c0 + PROJ_COLS])

    h_ref[...] = _rmsnorm(x_ref[...], g_pre_ref[...]).astype(jnp.bfloat16)

    half = CONV_WIDTH // 2
    for j in range(2):
        glu_ref[:, j * PROJ_COLS:(j + 1) * PROJ_COLS] = project(GLU_OFF + j * PROJ_COLS)
        for r0 in range(0, tile, CONV_ROWS):
            rows = slice(r0, r0 + CONV_ROWS)
            conv_x[0, CONV_HALO + r0:CONV_HALO + r0 + CONV_ROWS, j * half:(j + 1) * half] = (
                glu_ref[rows, j * PROJ_COLS:j * PROJ_COLS + half]
                * _sigmoid_of_half(glu_ref[rows, j * PROJ_COLS + half:(j + 1) * PROJ_COLS]))
        if j == 0:
            pool_ext[POOL_HALO:POOL_HALO + tile, :] = project(0)
        for c0 in range(j * half, (j + 1) * half, LANES):
            cols = slice(c0, c0 + LANES)
            x0 = conv_x[0, :, cols]
            for s in range(1, SUBLANES):
                conv_x[s, :, cols] = pltpu.roll(x0, conv_rows - s, axis=0)

    head_frame = t * tile + lax.broadcasted_iota(jnp.int32, (POOL_HALO, 1), 0)
    head_rows = slice(POOL_HALO, 2 * POOL_HALO)
    for g, window in enumerate(POOL_WINDOWS):
        cols = slice(g * POOL_GROUP_DIM, (g + 1) * POOL_GROUP_DIM)
        ext = pool_ext[:, cols]
        s = ext
        step = 1
        while step < window:
            s = s + pltpu.roll(s, step, axis=0)
            step *= 2
        inv_count = 1.0 / jnp.minimum(head_frame + 1, window).astype(jnp.float32)
        head = s[head_rows, :] * inv_count - ext[head_rows, :]
        body = s[2 * POOL_HALO:, :] * (1.0 / window) - ext[2 * POOL_HALO:, :]
        pooled_ref[0:POOL_HALO, cols] = head.astype(jnp.bfloat16)
        pooled_ref[POOL_HALO:, cols] = body.astype(jnp.bfloat16)

    first_tap = CONV_HALO - (CONV_K - 1)
    for r0 in range(0, tile, CONV_ROWS):
        accs = []
        for c0 in range(0, CONV_WIDTH, LANES):
            cols = slice(c0, c0 + LANES)
            acc = jnp.broadcast_to(conv_b_ref[:, cols], (CONV_ROWS, LANES))
            for k in range(CONV_K):
                q, s = divmod(first_tap + k, SUBLANES)
                lo = r0 + q * SUBLANES
                acc = acc + conv_w_ref[k:k + 1, cols] * conv_x[s, lo:lo + CONV_ROWS, cols]
            accs.append(acc)
        conv_out[r0:r0 + CONV_ROWS, :] = jnp.concatenate(accs, axis=-1)

    def gate_piece(i, after):
        lhs = h_ref[...]
        if after is not None:
            lhs = lhs + _repeat(_zero_after(after).astype(jnp.bfloat16), tile, D_MODEL)
        c0 = i * PROJ_COLS
        piece = _sigmoid_of_half(
            _dot(lhs, w_in_ref[:, GATE_A_OFF + c0:GATE_A_OFF + c0 + PROJ_COLS]))
        gate_ref[:, c0:c0 + PROJ_COLS] = piece
        return piece[tile - PACKED_ROWS:, PROJ_COLS - LANES:]

    half_ln_g = 0.5 * ln_g_ref[...]
    half_ln_b = 0.5 * ln_b_ref[...]

    def norm_stage(i, after):
        for r0 in range(i * stage_rows, (i + 1) * stage_rows, CONV_ROWS):
            c = conv_out[r0:r0 + CONV_ROWS, :]
            if after is not None:
                c = c + _repeat(_zero_after(after), CONV_ROWS, CONV_WIDTH)
            mu = jnp.mean(c, axis=-1, keepdims=True)
            cc = c - mu
            var = jnp.mean(cc * cc, axis=-1, keepdims=True)
            h = cc * lax.rsqrt(var + EPS) * half_ln_g + half_ln_b
            y = h * jnp.tanh(h) + h
            y_ref[r0:r0 + CONV_ROWS, :] = y.astype(jnp.bfloat16)
        return y[CONV_ROWS - PACKED_ROWS:, CONV_WIDTH - LANES:]

    n_pieces = 2 * D_MODEL // PROJ_COLS
    stage_rows = tile // n_pieces
    mxu_done = vpu_done = None
    for i in range(n_pieces):
        piece_done = gate_piece(i, vpu_done)
        vpu_done = norm_stage(i, mxu_done)
        mxu_done = piece_done

    for c0 in range(0, D_MODEL, PROJ_COLS):
        cols = slice(c0, c0 + PROJ_COLS)
        a = _dot(pooled_ref[...], w_pool_ref[:, cols])
        b = _dot(y_ref[...], w_conv_out_ref[:, cols])
        merged = (gate_ref[:, cols] * a
                  + gate_ref[:, D_MODEL + c0:D_MODEL + c0 + PROJ_COLS] * b)
        merged_ref[:, cols] = merged.astype(jnp.bfloat16)
    mix = _dot(merged_ref[...], w_o_ref[...])
    o_ref[...] = x_ref[...] + _rmsnorm(mix, g_post_ref[...])

    pool_ext[0:POOL_HALO, :] = pool_ext[tile:tile + POOL_HALO, :]
    conv_x[0, 0:CONV_HALO, 0:CONV_WIDTH] = conv_x[0, tile:tile + CONV_HALO, 0:CONV_WIDTH]


def _mlp_kernel(x_ref, g_pre_ref, w_up_hbm, w_down_hbm, g_post_ref, o_ref,
                w_up_ref, w_down_ref, stage_up, stage_down, sem):
    @pl.when(pl.program_id(0) == 0)
    def _():
        _load_as_bf16(w_up_hbm, w_up_ref, stage_up, sem, W_UP_CHUNK_ROWS)
        _load_as_bf16(w_down_hbm, w_down_ref, stage_down, sem, W_DOWN_CHUNK_ROWS)

    x = x_ref[...]
    h = _rmsnorm(x, g_pre_ref[...]).astype(jnp.bfloat16)
    acc = None
    for c0 in range(0, D_FF, FF_CHUNK):
        a = jnp.maximum(_dot(h, w_up_ref[:, c0:c0 + FF_CHUNK]), 0.0)
        p = _dot((a * a).astype(jnp.bfloat16), w_down_ref[c0:c0 + FF_CHUNK, 0:D_MODEL])
        acc = p if acc is None else acc + p
    o_ref[...] = x + _rmsnorm(acc, g_post_ref[...])


def _resident(shape):
    return pl.BlockSpec(shape, lambda *_: (0,) * len(shape), pipeline_mode=pl.Buffered(1))


def _mixer(x, g_pre, w_in, pool_w, pool_scale, w_pool_out, conv_w, layer, conv_b, ln_g, ln_b,
           w_conv_out, w_o, g_post):
    batch, seq, d = x.shape
    tile = MIXER_TILE
    assert seq % tile == 0 and d == D_MODEL
    x_spec = pl.BlockSpec((None, tile, d), lambda b, t: (b, t, 0))
    operands = (g_pre, w_in, pool_w, pool_scale, w_pool_out, conv_w, conv_b, ln_g, ln_b,
                w_conv_out, w_o, g_post)
    in_hbm = (w_in, w_pool_out, w_conv_out, w_o)
    hbm = pl.BlockSpec(memory_space=pl.ANY)
    conv_w_spec = pl.BlockSpec((None,) + conv_w.shape[1:], lambda *_: (layer, 0, 0),
                               pipeline_mode=pl.Buffered(1))

    def spec(a):
        if a is conv_w:
            return conv_w_spec
        return hbm if any(a is w for w in in_hbm) else _resident(a.shape)

    return pl.pallas_call(
        functools.partial(_mixer_kernel, tile=tile),
        grid=(batch, seq // tile),
        in_specs=[x_spec] + [spec(a) for a in operands],
        out_specs=x_spec,
        out_shape=jax.ShapeDtypeStruct(x.shape, x.dtype),
        scratch_shapes=[
            pltpu.VMEM(w_in.shape, jnp.bfloat16),
            pltpu.VMEM(w_pool_out.shape, jnp.bfloat16),
            pltpu.VMEM(w_conv_out.shape, jnp.bfloat16),
            pltpu.VMEM(w_o.shape, jnp.bfloat16),
            pltpu.VMEM((2, W_IN_CHUNK_ROWS, w_in.shape[1]), jnp.float32),
            pltpu.VMEM((2, W_OUT_CHUNK_ROWS, d), jnp.float32),
            pltpu.SemaphoreType.DMA((2,)),
            pltpu.VMEM((tile, d), jnp.bfloat16),
            pltpu.VMEM((POOL_HALO + tile, POOL_WIDTH), jnp.float32),
            pltpu.VMEM((tile, POOL_WIDTH), jnp.bfloat16),
            pltpu.VMEM((tile, 2 * CONV_WIDTH), jnp.float32),
            pltpu.VMEM((SUBLANES, CONV_HALO + tile, CONV_PITCH), jnp.float32),
            pltpu.VMEM((tile, CONV_WIDTH), jnp.float32),
            pltpu.VMEM((tile, CONV_WIDTH), jnp.bfloat16),
            pltpu.VMEM((tile, 2 * d), jnp.float32),
            pltpu.VMEM((tile, d), jnp.bfloat16),
        ],
        compiler_params=pltpu.CompilerParams(
            dimension_semantics=("arbitrary", "arbitrary"),
            vmem_limit_bytes=VMEM_LIMIT_BYTES),
        name="mixer",
    )(x, *operands)


def _mlp(x, g_pre, w_up, w_down, g_post):
    tokens, d = x.shape
    tile = MLP_TILE
    assert tokens % tile == 0 and d == D_MODEL
    x_spec = pl.BlockSpec((tile, d), lambda i: (i, 0))
    operands = (g_pre, w_up, w_down, g_post)
    return pl.pallas_call(
        _mlp_kernel,
        grid=(tokens // tile,),
        in_specs=[x_spec, _resident(g_pre.shape), pl.BlockSpec(memory_space=pl.ANY),
                  pl.BlockSpec(memory_space=pl.ANY), _resident(g_post.shape)],
        out_specs=x_spec,
        out_shape=jax.ShapeDtypeStruct(x.shape, x.dtype),
        scratch_shapes=[
            pltpu.VMEM((d, D_FF + LANES), jnp.bfloat16),
            pltpu.VMEM((D_FF, d + LANES), jnp.bfloat16),
            pltpu.VMEM((2, W_UP_CHUNK_ROWS, w_up.shape[1]), jnp.float32),
            pltpu.VMEM((2, W_DOWN_CHUNK_ROWS, w_down.shape[1]), jnp.float32),
            pltpu.SemaphoreType.DMA((2,)),
        ],
        compiler_params=pltpu.CompilerParams(
            dimension_semantics=("arbitrary",),
            vmem_limit_bytes=VMEM_LIMIT_BYTES),
        name="mlp",
    )(x, *operands)


def kernel(x, norm_mix_pre, w_in, pool_w, pool_scale, w_pool_out, conv_w, conv_b, conv_ln_g,
           conv_ln_b, w_conv_out, w_o, norm_mix_post, norm_mlp_pre, w_up, w_down, norm_mlp_post):
    batch, seq, d = x.shape
    row = lambda p: p.reshape(1, -1)
    for l in range(w_in.shape[0]):
        x = _mixer(x, row(norm_mix_pre[l]), w_in[l], pool_w[l],
                   pool_scale[l].reshape(len(POOL_WINDOWS), 1, POOL_GROUP_DIM), w_pool_out[l],
                   conv_w, l, row(conv_b[l]), row(conv_ln_g[l]), row(conv_ln_b[l]), w_conv_out[l],
                   w_o[l], row(norm_mix_post[l]))
        x = _mlp(x.reshape(batch * seq, d), row(norm_mlp_pre[l]), w_up[l], w_down[l],
                 row(norm_mlp_post[l])).reshape(batch, seq, d)
    return x
```

```python
import functools

import jax
import jax.numpy as jnp
from jax import lax
from jax.experimental import pallas as pl
from jax.experimental.pallas import tpu as pltpu

D_MODEL = 1024
POOL_WIDTH = 512
POOL_GROUP_DIM = 128
POOL_WINDOWS = (2, 4, 8, 16)
CONV_WIDTH = 512
CONV_K = 31
D_FF = 4 * D_MODEL
EPS = 1e-6

SUBLANES = 8
LANES = 128
POOL_HALO = 16
CONV_HALO = 32
CONV_PITCH = CONV_WIDTH + LANES
GLU_OFF = POOL_WIDTH
GATE_A_OFF = POOL_WIDTH + 2 * CONV_WIDTH
_GLU_HALF = CONV_WIDTH // 2
W_IN_COL_STARTS = (
    list(range(0, GLU_OFF, LANES))
    + [GLU_OFF + part * CONV_WIDTH + j * _GLU_HALF + c
       for j in range(2) for part in range(2) for c in range(0, _GLU_HALF, LANES)]
    + list(range(GATE_A_OFF, GATE_A_OFF + 2 * D_MODEL, LANES)))
W_IN_COL_SCALES = (
    [1.0] * (GLU_OFF // LANES) + [0.5] * ((2 * CONV_WIDTH + 2 * D_MODEL) // LANES))
MERGE_INPUT_SCALE = 0.5

MIXER_TILE = 512
MLP_TILE = 1024
FF_CHUNK = 1024
PROJ_COLS = 512
CONV_ROWS = 64
PACKED_ROWS = 16
W_IN_CHUNK_ROWS = 128
W_OUT_CHUNK_ROWS = 256
W_UP_CHUNK_ROWS = 128
W_DOWN_CHUNK_ROWS = 512
VMEM_LIMIT_BYTES = 56 * 1024 * 1024


def _dot(a, b):
    return jnp.dot(a, b, preferred_element_type=jnp.float32)


def _rmsnorm(x, g):
    ms = jnp.mean(x * x, axis=-1, keepdims=True)
    return x * lax.rsqrt(ms + EPS) * g


def _zero_after(v):
    return jnp.minimum(jnp.abs(v), 0.0)


def _repeat(v, rows, cols):
    v = jnp.concatenate([v] * (cols // v.shape[1]), axis=1)
    return jnp.concatenate([v] * (rows // v.shape[0]), axis=0)


def _stream_rows(src_hbm, stage, sem, chunk_rows, consume):
    n_chunks = src_hbm.shape[0] // chunk_rows

    def copy(c):
        return pltpu.make_async_copy(src_hbm.at[pl.ds(c * chunk_rows, chunk_rows), :],
                                     stage.at[c % 2], sem.at[c % 2])

    copy(0).start()
    for c in range(n_chunks):
        if c + 1 < n_chunks:
            copy(c + 1).start()
        copy(c).wait()
        consume(c, stage[c % 2])


def _load_as_bf16(src_hbm, dst_ref, stage, sem, chunk_rows, col_starts=None, col_scales=None,
                  scale=1.0):
    def cast(c, chunk):
        if scale != 1.0:
            chunk = chunk * scale
        if col_starts is not None:
            blocks = [chunk[:, c0:c0 + LANES] for c0 in col_starts]
            if col_scales is not None:
                blocks = [b if s == 1.0 else b * s for b, s in zip(blocks, col_scales)]
            chunk = jnp.concatenate(blocks, axis=1)
        dst_ref[c * chunk_rows:(c + 1) * chunk_rows, 0:chunk.shape[1]] = chunk.astype(jnp.bfloat16)

    _stream_rows(src_hbm, stage, sem, chunk_rows, cast)


def _load_folded_pool(w_pool_out_hbm, pool_w_ref, pool_scale_ref, dst_ref, stage, sem, chunk_rows):
    groups_per_chunk = chunk_rows // POOL_GROUP_DIM

    def fold(c, chunk):
        for j in range(groups_per_chunk):
            g = c * groups_per_chunk + j
            scaled = pool_w_ref[g] * pool_scale_ref[g]
            rows = chunk[j * POOL_GROUP_DIM:(j + 1) * POOL_GROUP_DIM, :]
            folded = jnp.dot(scaled, rows, precision=lax.Precision.HIGHEST,
                             preferred_element_type=jnp.float32)
            dst_ref[g * POOL_GROUP_DIM:(g + 1) * POOL_GROUP_DIM, :] = (
                folded * MERGE_INPUT_SCALE).astype(jnp.bfloat16)

    _stream_rows(w_pool_out_hbm, stage, sem, chunk_rows, fold)


def _mixer_kernel(x_ref, g_pre_ref, w_in_hbm, pool_w_ref, pool_scale_ref, w_pool_out_hbm,
                  conv_w_ref, conv_b_ref, ln_g_ref, ln_b_ref, w_conv_out_hbm, w_o_hbm,
                  g_post_ref, o_ref,
                  w_in_ref, w_pool_ref, w_conv_out_ref, w_o_ref, stage_in, stage_out, sem,
                  h_ref, pool_ext, pooled_ref, glu_ref, conv_x, conv_out, y_ref, gate_ref, merged_ref,
                  *, tile):
    t = pl.program_id(1)
    conv_rows = CONV_HALO + tile

    @pl.when((pl.program_id(0) == 0) & (t == 0))
    def _():
        _load_as_bf16(w_in_hbm, w_in_ref, stage_in, sem, W_IN_CHUNK_ROWS,
                      col_starts=W_IN_COL_STARTS, col_scales=W_IN_COL_SCALES)
        _load_folded_pool(w_pool_out_hbm, pool_w_ref, pool_scale_ref, w_pool_ref, stage_out, sem,
                          W_OUT_CHUNK_ROWS)
        _load_as_bf16(w_conv_out_hbm, w_conv_out_ref, stage_out, sem, W_OUT_CHUNK_ROWS,
                      scale=MERGE_INPUT_SCALE)
        _load_as_bf16(w_o_hbm, w_o_ref, stage_out, sem, W_OUT_CHUNK_ROWS)

    @pl.when(t == 0)
    def _():
        pool_ext[0:POOL_HALO, :] = jnp.zeros((POOL_HALO, POOL_WIDTH), jnp.float32)
        conv_x[0, 0:CONV_HALO, 0:CONV_WIDTH] = jnp.zeros((CONV_HALO, CONV_WIDTH), jnp.float32)

    def project(c0):
        return _dot(h_ref[...], w_in_ref[:, c0:c0 + PROJ_COLS])

    h_ref[...] = _rmsnorm(x_ref[...], g_pre_ref[...]).astype(jnp.bfloat16)

    half = CONV_WIDTH // 2
    for j in range(2):
        glu_ref[:, j * PROJ_COLS:(j + 1) * PROJ_COLS] = project(GLU_OFF + j * PROJ_COLS)
        for r0 in range(0, tile, CONV_ROWS):
            rows = slice(r0, r0 + CONV_ROWS)
            conv_x[0, CONV_HALO + r0:CONV_HALO + r0 + CONV_ROWS, j * half:(j + 1) * half] = (
                glu_ref[rows, j * PROJ_COLS:j * PROJ_COLS + half]
                * (jnp.tanh(glu_ref[rows, j * PROJ_COLS + half:(j + 1) * PROJ_COLS]) + 1.0))
        if j == 0:
            pool_ext[POOL_HALO:POOL_HALO + tile, :] = project(0)
        for c0 in range(j * half, (j + 1) * half, LANES):
            cols = slice(c0, c0 + LANES)
            x0 = conv_x[0, :, cols]
            for s in range(1, SUBLANES):
                conv_x[s, :, cols] = pltpu.roll(x0, conv_rows - s, axis=0)

    head_frame = t * tile + lax.broadcasted_iota(jnp.int32, (POOL_HALO, 1), 0)
    head_rows = slice(POOL_HALO, 2 * POOL_HALO)
    for g, window in enumerate(POOL_WINDOWS):
        cols = slice(g * POOL_GROUP_DIM, (g + 1) * POOL_GROUP_DIM)
        ext = pool_ext[:, cols]
        s = ext
        step = 1
        while step < window:
            s = s + pltpu.roll(s, step, axis=0)
            step *= 2
        inv_count = 1.0 / jnp.minimum(head_frame + 1, window).astype(jnp.float32)
        head = s[head_rows, :] * inv_count - ext[head_rows, :]
        body = s[2 * POOL_HALO:, :] * (1.0 / window) - ext[2 * POOL_HALO:, :]
        pooled_ref[0:POOL_HALO, cols] = head.astype(jnp.bfloat16)
        pooled_ref[POOL_HALO:, cols] = body.astype(jnp.bfloat16)

    first_tap = CONV_HALO - (CONV_K - 1)
    for r0 in range(0, tile, CONV_ROWS):
        accs = []
        for c0 in range(0, CONV_WIDTH, LANES):
            cols = slice(c0, c0 + LANES)
            acc = jnp.broadcast_to(conv_b_ref[:, cols], (CONV_ROWS, LANES))
            for k in range(CONV_K):
                q, s = divmod(first_tap + k, SUBLANES)
                lo = r0 + q * SUBLANES
                acc = acc + conv_w_ref[k:k + 1, cols] * conv_x[s, lo:lo + CONV_ROWS, cols]
            accs.append(acc)
        conv_out[r0:r0 + CONV_ROWS, :] = jnp.concatenate(accs, axis=-1)

    def gate_piece(i, after):
        lhs = h_ref[...]
        if after is not None:
            lhs = lhs + _repeat(_zero_after(after).astype(jnp.bfloat16), tile, D_MODEL)
        c0 = i * PROJ_COLS
        piece = jnp.tanh(_dot(lhs, w_in_ref[:, GATE_A_OFF + c0:GATE_A_OFF + c0 + PROJ_COLS]))
        gate_ref[:, c0:c0 + PROJ_COLS] = piece
        return piece[tile - PACKED_ROWS:, PROJ_COLS - LANES:]

    half_ln_g = 0.5 * ln_g_ref[...]
    half_ln_b = 0.5 * ln_b_ref[...]

    def norm_stage(i, after):
        for r0 in range(i * stage_rows, (i + 1) * stage_rows, CONV_ROWS):
            c = conv_out[r0:r0 + CONV_ROWS, :]
            if after is not None:
                c = c + _repeat(_zero_after(after), CONV_ROWS, CONV_WIDTH)
            mu = jnp.mean(c, axis=-1, keepdims=True)
            cc = c - mu
            var = jnp.mean(cc * cc, axis=-1, keepdims=True)
            h = cc * lax.rsqrt(var + EPS) * half_ln_g + half_ln_b
            y = h * jnp.tanh(h) + h
            y_ref[r0:r0 + CONV_ROWS, :] = y.astype(jnp.bfloat16)
        return y[CONV_ROWS - PACKED_ROWS:, CONV_WIDTH - LANES:]

    n_pieces = 2 * D_MODEL // PROJ_COLS
    stage_rows = tile // n_pieces
    mxu_done = vpu_done = None
    for i in range(n_pieces):
        piece_done = gate_piece(i, vpu_done)
        vpu_done = norm_stage(i, mxu_done)
        mxu_done = piece_done

    for c0 in range(0, D_MODEL, PROJ_COLS):
        cols = slice(c0, c0 + PROJ_COLS)
        a = _dot(pooled_ref[...], w_pool_ref[:, cols])
        b = _dot(y_ref[...], w_conv_out_ref[:, cols])
        merged = ((gate_ref[:, cols] + 1.0) * a
                  + (gate_ref[:, D_MODEL + c0:D_MODEL + c0 + PROJ_COLS] + 1.0) * b)
        merged_ref[:, cols] = merged.astype(jnp.bfloat16)
    mix = _dot(merged_ref[...], w_o_ref[...])
    o_ref[...] = x_ref[...] + _rmsnorm(mix, g_post_ref[...])

    pool_ext[0:POOL_HALO, :] = pool_ext[tile:tile + POOL_HALO, :]
    conv_x[0, 0:CONV_HALO, 0:CONV_WIDTH] = conv_x[0, tile:tile + CONV_HALO, 0:CONV_WIDTH]


def _mlp_kernel(x_ref, g_pre_ref, w_up_hbm, w_down_hbm, g_post_ref, o_ref,
                w_up_ref, w_down_ref, stage_up, stage_down, sem):
    @pl.when(pl.program_id(0) == 0)
    def _():
        _load_as_bf16(w_up_hbm, w_up_ref, stage_up, sem, W_UP_CHUNK_ROWS)
        _load_as_bf16(w_down_hbm, w_down_ref, stage_down, sem, W_DOWN_CHUNK_ROWS)

    x = x_ref[...]
    h = _rmsnorm(x, g_pre_ref[...]).astype(jnp.bfloat16)
    acc = None
    for c0 in range(0, D_FF, FF_CHUNK):
        a = jnp.maximum(_dot(h, w_up_ref[:, c0:c0 + FF_CHUNK]), 0.0)
        p = _dot((a * a).astype(jnp.bfloat16), w_down_ref[c0:c0 + FF_CHUNK, 0:D_MODEL])
        acc = p if acc is None else acc + p
    o_ref[...] = x + _rmsnorm(acc, g_post_ref[...])


def _resident(shape):
    return pl.BlockSpec(shape, lambda *_: (0,) * len(shape), pipeline_mode=pl.Buffered(1))


def _mixer(x, g_pre, w_in, pool_w, pool_scale, w_pool_out, conv_w, layer, conv_b, ln_g, ln_b,
           w_conv_out, w_o, g_post):
    batch, seq, d = x.shape
    tile = MIXER_TILE
    assert seq % tile == 0 and d == D_MODEL
    x_spec = pl.BlockSpec((None, tile, d), lambda b, t: (b, t, 0))
    operands = (g_pre, w_in, pool_w, pool_scale, w_pool_out, conv_w, conv_b, ln_g, ln_b,
                w_conv_out, w_o, g_post)
    in_hbm = (w_in, w_pool_out, w_conv_out, w_o)
    hbm = pl.BlockSpec(memory_space=pl.ANY)
    conv_w_spec = pl.BlockSpec((None,) + conv_w.shape[1:], lambda *_: (layer, 0, 0),
                               pipeline_mode=pl.Buffered(1))

    def spec(a):
        if a is conv_w:
            return conv_w_spec
        return hbm if any(a is w for w in in_hbm) else _resident(a.shape)

    return pl.pallas_call(
        functools.partial(_mixer_kernel, tile=tile),
        grid=(batch, seq // tile),
        in_specs=[x_spec] + [spec(a) for a in operands],
        out_specs=x_spec,
        out_shape=jax.ShapeDtypeStruct(x.shape, x.dtype),
        scratch_shapes=[
            pltpu.VMEM(w_in.shape, jnp.bfloat16),
            pltpu.VMEM(w_pool_out.shape, jnp.bfloat16),
            pltpu.VMEM(w_conv_out.shape, jnp.bfloat16),
            pltpu.VMEM(w_o.shape, jnp.bfloat16),
            pltpu.VMEM((2, W_IN_CHUNK_ROWS, w_in.shape[1]), jnp.float32),
            pltpu.VMEM((2, W_OUT_CHUNK_ROWS, d), jnp.float32),
            pltpu.SemaphoreType.DMA((2,)),
            pltpu.VMEM((tile, d), jnp.bfloat16),
            pltpu.VMEM((POOL_HALO + tile, POOL_WIDTH), jnp.float32),
            pltpu.VMEM((tile, POOL_WIDTH), jnp.bfloat16),
            pltpu.VMEM((tile, 2 * CONV_WIDTH), jnp.float32),
            pltpu.VMEM((SUBLANES, CONV_HALO + tile, CONV_PITCH), jnp.float32),
            pltpu.VMEM((tile, CONV_WIDTH), jnp.float32),
            pltpu.VMEM((tile, CONV_WIDTH), jnp.bfloat16),
            pltpu.VMEM((tile, 2 * d), jnp.float32),
            pltpu.VMEM((tile, d), jnp.bfloat16),
        ],
        compiler_params=pltpu.CompilerParams(
            dimension_semantics=("arbitrary", "arbitrary"),
            vmem_limit_bytes=VMEM_LIMIT_BYTES),
        name="mixer",
    )(x, *operands)


def _mlp(x, g_pre, w_up, w_down, g_post):
    tokens, d = x.shape
    tile = MLP_TILE
    assert tokens % tile == 0 and d == D_MODEL
    x_spec = pl.BlockSpec((tile, d), lambda i: (i, 0))
    operands = (g_pre, w_up, w_down, g_post)
    return pl.pallas_call(
        _mlp_kernel,
        grid=(tokens // tile,),
        in_specs=[x_spec, _resident(g_pre.shape), pl.BlockSpec(memory_space=pl.ANY),
                  pl.BlockSpec(memory_space=pl.ANY), _resident(g_post.shape)],
        out_specs=x_spec,
        out_shape=jax.ShapeDtypeStruct(x.shape, x.dtype),
        scratch_shapes=[
            pltpu.VMEM((d, D_FF + LANES), jnp.bfloat16),
            pltpu.VMEM((D_FF, d + LANES), jnp.bfloat16),
            pltpu.VMEM((2, W_UP_CHUNK_ROWS, w_up.shape[1]), jnp.float32),
            pltpu.VMEM((2, W_DOWN_CHUNK_ROWS, w_down.shape[1]), jnp.float32),
            pltpu.SemaphoreType.DMA((2,)),
        ],
        compiler_params=pltpu.CompilerParams(
            dimension_semantics=("arbitrary",),
            vmem_limit_bytes=VMEM_LIMIT_BYTES),
        name="mlp",
    )(x, *operands)


def kernel(x, norm_mix_pre, w_in, pool_w, pool_scale, w_pool_out, conv_w, conv_b, conv_ln_g,
           conv_ln_b, w_conv_out, w_o, norm_mix_post, norm_mlp_pre, w_up, w_down, norm_mlp_post):
    batch, seq, d = x.shape
    row = lambda p: p.reshape(1, -1)
    for l in range(w_in.shape[0]):
        x = _mixer(x, row(norm_mix_pre[l]), w_in[l], pool_w[l],
                   pool_scale[l].reshape(len(POOL_WINDOWS), 1, POOL_GROUP_DIM), w_pool_out[l],
                   conv_w, l, row(conv_b[l]), row(conv_ln_g[l]), row(conv_ln_b[l]), w_conv_out[l],
                   w_o[l], row(norm_mix_post[l]))
        x = _mlp(x.reshape(batch * seq, d), row(norm_mlp_pre[l]), w_up[l], w_down[l],
                 row(norm_mlp_post[l])).reshape(batch, seq, d)
    return x
```

```python
import functools

import jax
import jax.numpy as jnp
from jax import lax
from jax.experimental import pallas as pl
from jax.experimental.pallas import tpu as pltpu

D_MODEL = 1024
POOL_WIDTH = 512
POOL_GROUP_DIM = 128
POOL_WINDOWS = (2, 4, 8, 16)
CONV_WIDTH = 512
CONV_K = 31
D_FF = 4 * D_MODEL
EPS = 1e-6

SUBLANES = 8
LANES = 128
POOL_HALO = 16
CONV_HALO = 32
CONV_PITCH = CONV_WIDTH + LANES
GLU_OFF = POOL_WIDTH
GATE_A_OFF = POOL_WIDTH + 2 * CONV_WIDTH
_GLU_HALF = CONV_WIDTH // 2
W_IN_COL_STARTS = (
    list(range(0, GLU_OFF, LANES))
    + [GLU_OFF + part * CONV_WIDTH + j * _GLU_HALF + c
       for j in range(2) for part in range(2) for c in range(0, _GLU_HALF, LANES)]
    + list(range(GATE_A_OFF, GATE_A_OFF + 2 * D_MODEL, LANES)))
W_IN_COL_SCALES = (
    [1.0] * (GLU_OFF // LANES) + [0.5] * ((2 * CONV_WIDTH + 2 * D_MODEL) // LANES))
MERGE_INPUT_SCALE = 0.5

MIXER_TILE = 512
MLP_TILE = 1024
FF_CHUNK = 1024
PROJ_COLS = 512
CONV_ROWS = 64
PACKED_ROWS = 16
W_IN_CHUNK_ROWS = 128
W_OUT_CHUNK_ROWS = 256
W_UP_CHUNK_ROWS = 128
W_DOWN_CHUNK_ROWS = 512
VMEM_LIMIT_BYTES = 56 * 1024 * 1024


def _dot(a, b):
    return jnp.dot(a, b, preferred_element_type=jnp.float32)


def _rmsnorm(x, g):
    ms = jnp.mean(x * x, axis=-1, keepdims=True)
    return x * lax.rsqrt(ms + EPS) * g


def _zero_after(v):
    return jnp.minimum(jnp.abs(v), 0.0)


def _repeat(v, rows, cols):
    v = jnp.concatenate([v] * (cols // v.shape[1]), axis=1)
    return jnp.concatenate([v] * (rows // v.shape[0]), axis=0)


def _stream_rows(src_hbm, stage, sem, chunk_rows, consume):
    n_chunks = src_hbm.shape[0] // chunk_rows

    def copy(c):
        return pltpu.make_async_copy(src_hbm.at[pl.ds(c * chunk_rows, chunk_rows), :],
                                     stage.at[c % 2], sem.at[c % 2])

    copy(0).start()
    for c in range(n_chunks):
        if c + 1 < n_chunks:
            copy(c + 1).start()
        copy(c).wait()
        consume(c, stage[c % 2])


def _load_as_bf16(src_hbm, dst_ref, stage, sem, chunk_rows, col_starts=None, col_scales=None,
                  scale=1.0):
    def cast(c, chunk):
        if scale != 1.0:
            chunk = chunk * scale
        if col_starts is not None:
            blocks = [chunk[:, c0:c0 + LANES] for c0 in col_starts]
            if col_scales is not None:
                blocks = [b if s == 1.0 else b * s for b, s in zip(blocks, col_scales)]
            chunk = jnp.concatenate(blocks, axis=1)
        dst_ref[c * chunk_rows:(c + 1) * chunk_rows, 0:chunk.shape[1]] = chunk.astype(jnp.bfloat16)

    _stream_rows(src_hbm, stage, sem, chunk_rows, cast)


def _load_folded_pool(w_pool_out_hbm, pool_w_ref, pool_scale_ref, dst_ref, stage, sem, chunk_rows):
    groups_per_chunk = chunk_rows // POOL_GROUP_DIM

    def fold(c, chunk):
        for j in range(groups_per_chunk):
            g = c * groups_per_chunk + j
            scaled = pool_w_ref[g] * pool_scale_ref[g]
            rows = chunk[j * POOL_GROUP_DIM:(j + 1) * POOL_GROUP_DIM, :]
            folded = jnp.dot(scaled, rows, precision=lax.Precision.HIGHEST,
                             preferred_element_type=jnp.float32)
            dst_ref[g * POOL_GROUP_DIM:(g + 1) * POOL_GROUP_DIM, :] = (
                folded * MERGE_INPUT_SCALE).astype(jnp.bfloat16)

    _stream_rows(w_pool_out_hbm, stage, sem, chunk_rows, fold)


def _mixer_kernel(x_ref, g_pre_ref, w_in_hbm, pool_w_ref, pool_scale_ref, w_pool_out_hbm,
                  conv_w_ref, conv_b_ref, ln_g_ref, ln_b_ref, w_conv_out_hbm, w_o_hbm,
                  g_post_ref, o_ref,
                  w_in_ref, w_pool_ref, w_conv_out_ref, w_o_ref, stage_in, stage_out, sem,
                  h_ref, pool_ext, pooled_ref, glu_ref, conv_x, conv_out, y_ref, gate_ref, merged_ref,
                  ab_ref, *, tile):
    t = pl.program_id(1)
    conv_rows = CONV_HALO + tile

    @pl.when((pl.program_id(0) == 0) & (t == 0))
    def _():
        _load_as_bf16(w_in_hbm, w_in_ref, stage_in, sem, W_IN_CHUNK_ROWS,
                      col_starts=W_IN_COL_STARTS, col_scales=W_IN_COL_SCALES)
        _load_folded_pool(w_pool_out_hbm, pool_w_ref, pool_scale_ref, w_pool_ref, stage_out, sem,
                          W_OUT_CHUNK_ROWS)
        _load_as_bf16(w_conv_out_hbm, w_conv_out_ref, stage_out, sem, W_OUT_CHUNK_ROWS,
                      scale=MERGE_INPUT_SCALE)
        _load_as_bf16(w_o_hbm, w_o_ref, stage_out, sem, W_OUT_CHUNK_ROWS)

    @pl.when(t == 0)
    def _():
        pool_ext[0:POOL_HALO, :] = jnp.zeros((POOL_HALO, POOL_WIDTH), jnp.float32)
        conv_x[0, 0:CONV_HALO, 0:CONV_WIDTH] = jnp.zeros((CONV_HALO, CONV_WIDTH), jnp.float32)

    def project(c0):
        return _dot(h_ref[...], w_in_ref[:, c0:c0 + PROJ_COLS])

    h_ref[...] = _rmsnorm(x_ref[...], g_pre_ref[...]).astype(jnp.bfloat16)

    half = CONV_WIDTH // 2
    for j in range(2):
        glu_ref[:, j * PROJ_COLS:(j + 1) * PROJ_COLS] = project(GLU_OFF + j * PROJ_COLS)
        for r0 in range(0, tile, CONV_ROWS):
            rows = slice(r0, r0 + CONV_ROWS)
            conv_x[0, CONV_HALO + r0:CONV_HALO + r0 + CONV_ROWS, j * half:(j + 1) * half] = (
                glu_ref[rows, j * PROJ_COLS:j * PROJ_COLS + half]
                * (jnp.tanh(glu_ref[rows, j * PROJ_COLS + half:(j + 1) * PROJ_COLS]) + 1.0))
        if j == 0:
            pool_ext[POOL_HALO:POOL_HALO + tile, :] = project(0)
        for c0 in range(j * half, (j + 1) * half, LANES):
            cols = slice(c0, c0 + LANES)
            x0 = conv_x[0, :, cols]
            for s in range(1, SUBLANES):
                conv_x[s, :, cols] = pltpu.roll(x0, conv_rows - s, axis=0)

    head_frame = t * tile + lax.broadcasted_iota(jnp.int32, (POOL_HALO, 1), 0)
    head_rows = slice(POOL_HALO, 2 * POOL_HALO)
    for g, window in enumerate(POOL_WINDOWS):
        cols = slice(g * POOL_GROUP_DIM, (g + 1) * POOL_GROUP_DIM)
        ext = pool_ext[:, cols]
        s = ext
        step = 1
        while step < window:
            s = s + pltpu.roll(s, step, axis=0)
            step *= 2
        inv_count = 1.0 / jnp.minimum(head_frame + 1, window).astype(jnp.float32)
        head = s[head_rows, :] * inv_count - ext[head_rows, :]
        body = s[2 * POOL_HALO:, :] * (1.0 / window) - ext[2 * POOL_HALO:, :]
        pooled_ref[0:POOL_HALO, cols] = head.astype(jnp.bfloat16)
        pooled_ref[POOL_HALO:, cols] = body.astype(jnp.bfloat16)

    first_tap = CONV_HALO - (CONV_K - 1)
    for r0 in range(0, tile, CONV_ROWS):
        accs = []
        for c0 in range(0, CONV_WIDTH, LANES):
            cols = slice(c0, c0 + LANES)
            acc = jnp.broadcast_to(conv_b_ref[:, cols], (CONV_ROWS, LANES))
            for k in range(CONV_K):
                q, s = divmod(first_tap + k, SUBLANES)
                lo = r0 + q * SUBLANES
                acc = acc + conv_w_ref[k:k + 1, cols] * conv_x[s, lo:lo + CONV_ROWS, cols]
            accs.append(acc)
        conv_out[r0:r0 + CONV_ROWS, :] = jnp.concatenate(accs, axis=-1)

    def gate_piece(i, after):
        lhs = h_ref[...]
        if after is not None:
            lhs = lhs + _repeat(_zero_after(after).astype(jnp.bfloat16), tile, D_MODEL)
        c0 = i * PROJ_COLS
        piece = jnp.tanh(_dot(lhs, w_in_ref[:, GATE_A_OFF + c0:GATE_A_OFF + c0 + PROJ_COLS]))
        gate_ref[:, c0:c0 + PROJ_COLS] = piece
        return piece[tile - PACKED_ROWS:, PROJ_COLS - LANES:]

    half_ln_g = 0.5 * ln_g_ref[...]
    half_ln_b = 0.5 * ln_b_ref[...]

    def norm_stage(i, after):
        for r0 in range(i * stage_rows, (i + 1) * stage_rows, CONV_ROWS):
            c = conv_out[r0:r0 + CONV_ROWS, :]
            if after is not None:
                c = c + _repeat(_zero_after(after), CONV_ROWS, CONV_WIDTH)
            mu = jnp.mean(c, axis=-1, keepdims=True)
            cc = c - mu
            var = jnp.mean(cc * cc, axis=-1, keepdims=True)
            h = cc * lax.rsqrt(var + EPS) * half_ln_g + half_ln_b
            y = h * jnp.tanh(h) + h
            y_ref[r0:r0 + CONV_ROWS, :] = y.astype(jnp.bfloat16)
        return y[CONV_ROWS - PACKED_ROWS:, CONV_WIDTH - LANES:]

    n_pieces = 2 * D_MODEL // PROJ_COLS
    stage_rows = tile // n_pieces
    mxu_done = vpu_done = None
    for i in range(n_pieces):
        piece_done = gate_piece(i, vpu_done)
        vpu_done = norm_stage(i, mxu_done)
        mxu_done = piece_done

    for c0 in range(0, D_MODEL, PROJ_COLS):
        cols = slice(c0, c0 + PROJ_COLS)
        ia, ib = 2 * (c0 // PROJ_COLS), 2 * (c0 // PROJ_COLS) + 1
        ab_ref[ia] = _dot(pooled_ref[...], w_pool_ref[:, cols])
        ab_ref[ib] = _dot(y_ref[...], w_conv_out_ref[:, cols])
        for r0 in range(0, tile, CONV_ROWS):
            rows = slice(r0, r0 + CONV_ROWS)
            merged = ((gate_ref[rows, cols] + 1.0) * ab_ref[ia, rows, :]
                      + (gate_ref[rows, D_MODEL + c0:D_MODEL + c0 + PROJ_COLS] + 1.0)
                      * ab_ref[ib, rows, :])
            merged_ref[rows, cols] = merged.astype(jnp.bfloat16)
    mix = _dot(merged_ref[...], w_o_ref[...])
    o_ref[...] = x_ref[...] + _rmsnorm(mix, g_post_ref[...])

    pool_ext[0:POOL_HALO, :] = pool_ext[tile:tile + POOL_HALO, :]
    conv_x[0, 0:CONV_HALO, 0:CONV_WIDTH] = conv_x[0, tile:tile + CONV_HALO, 0:CONV_WIDTH]


def _mlp_kernel(x_ref, g_pre_ref, w_up_hbm, w_down_hbm, g_post_ref, o_ref,
                w_up_ref, w_down_ref, stage_up, stage_down, sem):
    @pl.when(pl.program_id(0) == 0)
    def _():
        _load_as_bf16(w_up_hbm, w_up_ref, stage_up, sem, W_UP_CHUNK_ROWS)
        _load_as_bf16(w_down_hbm, w_down_ref, stage_down, sem, W_DOWN_CHUNK_ROWS)

    x = x_ref[...]
    h = _rmsnorm(x, g_pre_ref[...]).astype(jnp.bfloat16)
    acc = None
    for c0 in range(0, D_FF, FF_CHUNK):
        a = jnp.maximum(_dot(h, w_up_ref[:, c0:c0 + FF_CHUNK]), 0.0)
        p = _dot((a * a).astype(jnp.bfloat16), w_down_ref[c0:c0 + FF_CHUNK, 0:D_MODEL])
        acc = p if acc is None else acc + p
    o_ref[...] = x + _rmsnorm(acc, g_post_ref[...])


def _resident(shape):
    return pl.BlockSpec(shape, lambda *_: (0,) * len(shape), pipeline_mode=pl.Buffered(1))


def _mixer(x, g_pre, w_in, pool_w, pool_scale, w_pool_out, conv_w, layer, conv_b, ln_g, ln_b,
           w_conv_out, w_o, g_post):
    batch, seq, d = x.shape
    tile = MIXER_TILE
    assert seq % tile == 0 and d == D_MODEL
    x_spec = pl.BlockSpec((None, tile, d), lambda b, t: (b, t, 0))
    operands = (g_pre, w_in, pool_w, pool_scale, w_pool_out, conv_w, conv_b, ln_g, ln_b,
                w_conv_out, w_o, g_post)
    in_hbm = (w_in, w_pool_out, w_conv_out, w_o)
    hbm = pl.BlockSpec(memory_space=pl.ANY)
    conv_w_spec = pl.BlockSpec((None,) + conv_w.shape[1:], lambda *_: (layer, 0, 0),
                               pipeline_mode=pl.Buffered(1))

    def spec(a):
        if a is conv_w:
            return conv_w_spec
        return hbm if any(a is w for w in in_hbm) else _resident(a.shape)

    return pl.pallas_call(
        functools.partial(_mixer_kernel, tile=tile),
        grid=(batch, seq // tile),
        in_specs=[x_spec] + [spec(a) for a in operands],
        out_specs=x_spec,
        out_shape=jax.ShapeDtypeStruct(x.shape, x.dtype),
        scratch_shapes=[
            pltpu.VMEM(w_in.shape, jnp.bfloat16),
            pltpu.VMEM(w_pool_out.shape, jnp.bfloat16),
            pltpu.VMEM(w_conv_out.shape, jnp.bfloat16),
            pltpu.VMEM(w_o.shape, jnp.bfloat16),
            pltpu.VMEM((2, W_IN_CHUNK_ROWS, w_in.shape[1]), jnp.float32),
            pltpu.VMEM((2, W_OUT_CHUNK_ROWS, d), jnp.float32),
            pltpu.SemaphoreType.DMA((2,)),
            pltpu.VMEM((tile, d), jnp.bfloat16),
            pltpu.VMEM((POOL_HALO + tile, POOL_WIDTH), jnp.float32),
            pltpu.VMEM((tile, POOL_WIDTH), jnp.bfloat16),
            pltpu.VMEM((tile, 2 * CONV_WIDTH), jnp.float32),
            pltpu.VMEM((SUBLANES, CONV_HALO + tile, CONV_PITCH), jnp.float32),
            pltpu.VMEM((tile, CONV_WIDTH), jnp.float32),
            pltpu.VMEM((tile, CONV_WIDTH), jnp.bfloat16),
            pltpu.VMEM((tile, 2 * d), jnp.float32),
            pltpu.VMEM((tile, d), jnp.bfloat16),
            pltpu.VMEM((2 * d // PROJ_COLS, tile, PROJ_COLS), jnp.float32),
        ],
        compiler_params=pltpu.CompilerParams(
            dimension_semantics=("arbitrary", "arbitrary"),
            vmem_limit_bytes=VMEM_LIMIT_BYTES),
        name="mixer",
    )(x, *operands)


def _mlp(x, g_pre, w_up, w_down, g_post):
    tokens, d = x.shape
    tile = MLP_TILE
    assert tokens % tile == 0 and d == D_MODEL
    x_spec = pl.BlockSpec((tile, d), lambda i: (i, 0))
    operands = (g_pre, w_up, w_down, g_post)
    return pl.pallas_call(
        _mlp_kernel,
        grid=(tokens // tile,),
        in_specs=[x_spec, _resident(g_pre.shape), pl.BlockSpec(memory_space=pl.ANY),
                  pl.BlockSpec(memory_space=pl.ANY), _resident(g_post.shape)],
        out_specs=x_spec,
        out_shape=jax.ShapeDtypeStruct(x.shape, x.dtype),
        scratch_shapes=[
            pltpu.VMEM((d, D_FF + LANES), jnp.bfloat16),
            pltpu.VMEM((D_FF, d + LANES), jnp.bfloat16),
            pltpu.VMEM((2, W_UP_CHUNK_ROWS, w_up.shape[1]), jnp.float32),
            pltpu.VMEM((2, W_DOWN_CHUNK_ROWS, w_down.shape[1]), jnp.float32),
            pltpu.SemaphoreType.DMA((2,)),
        ],
        compiler_params=pltpu.CompilerParams(
            dimension_semantics=("arbitrary",),
            vmem_limit_bytes=VMEM_LIMIT_BYTES),
        name="mlp",
    )(x, *operands)


def kernel(x, norm_mix_pre, w_in, pool_w, pool_scale, w_pool_out, conv_w, conv_b, conv_ln_g,
           conv_ln_b, w_conv_out, w_o, norm_mix_post, norm_mlp_pre, w_up, w_down, norm_mlp_post):
    batch, seq, d = x.shape
    row = lambda p: p.reshape(1, -1)
    for l in range(w_in.shape[0]):
        x = _mixer(x, row(norm_mix_pre[l]), w_in[l], pool_w[l],
                   pool_scale[l].reshape(len(POOL_WINDOWS), 1, POOL_GROUP_DIM), w_pool_out[l],
                   conv_w, l, row(conv_b[l]), row(conv_ln_g[l]), row(conv_ln_b[l]), w_conv_out[l],
                   w_o[l], row(norm_mix_post[l]))
        x = _mlp(x.reshape(batch * seq, d), row(norm_mlp_pre[l]), w_up[l], w_down[l],
                 row(norm_mlp_post[l])).reshape(batch, seq, d)
    return x
```

```python
import functools

import jax
import jax.numpy as jnp
from jax import lax
from jax.experimental import pallas as pl
from jax.experimental.pallas import tpu as pltpu

D_MODEL = 1024
POOL_WIDTH = 512
POOL_GROUP_DIM = 128
POOL_WINDOWS = (2, 4, 8, 16)
CONV_WIDTH = 512
CONV_K = 31
D_FF = 4 * D_MODEL
EPS = 1e-6

SUBLANES = 8
LANES = 128
POOL_HALO = 16
CONV_HALO = 32
CONV_PITCH = CONV_WIDTH + LANES
GLU_OFF = POOL_WIDTH
GATE_A_OFF = POOL_WIDTH + 2 * CONV_WIDTH
_GLU_HALF = CONV_WIDTH // 2
W_IN_COL_STARTS = (
    list(range(0, GLU_OFF, LANES))
    + [GLU_OFF + part * CONV_WIDTH + j * _GLU_HALF + c
       for j in range(2) for part in range(2) for c in range(0, _GLU_HALF, LANES)]
    + list(range(GATE_A_OFF, GATE_A_OFF + 2 * D_MODEL, LANES)))
W_IN_COL_SCALES = (
    [1.0] * (GLU_OFF // LANES) + [0.5] * ((2 * CONV_WIDTH + 2 * D_MODEL) // LANES))
MERGE_INPUT_SCALE = 0.5

MIXER_TILE = 512
MLP_TILE = 1024
FF_CHUNK = 1024
PROJ_COLS = 512
CONV_ROWS = 64
PACKED_ROWS = 16
W_IN_CHUNK_ROWS = 128
W_OUT_CHUNK_ROWS = 256
W_UP_CHUNK_ROWS = 128
W_DOWN_CHUNK_ROWS = 512
VMEM_LIMIT_BYTES = 56 * 1024 * 1024


def _dot(a, b):
    return jnp.dot(a, b, preferred_element_type=jnp.float32)


def _rmsnorm(x, g):
    ms = jnp.mean(x * x, axis=-1, keepdims=True)
    return x * lax.rsqrt(ms + EPS) * g


def _zero_after(v):
    return jnp.minimum(jnp.abs(v), 0.0)


def _repeat(v, rows, cols):
    v = jnp.concatenate([v] * (cols // v.shape[1]), axis=1)
    return jnp.concatenate([v] * (rows // v.shape[0]), axis=0)


def _stream_rows(src_hbm, stage, sem, chunk_rows, consume):
    n_chunks = src_hbm.shape[0] // chunk_rows

    def copy(c):
        return pltpu.make_async_copy(src_hbm.at[pl.ds(c * chunk_rows, chunk_rows), :],
                                     stage.at[c % 2], sem.at[c % 2])

    copy(0).start()
    for c in range(n_chunks):
        if c + 1 < n_chunks:
            copy(c + 1).start()
        copy(c).wait()
        consume(c, stage[c % 2])


def _load_as_bf16(src_hbm, dst_ref, stage, sem, chunk_rows, col_starts=None, col_scales=None,
                  scale=1.0):
    def cast(c, chunk):
        if scale != 1.0:
            chunk = chunk * scale
        if col_starts is not None:
            blocks = [chunk[:, c0:c0 + LANES] for c0 in col_starts]
            if col_scales is not None:
                blocks = [b if s == 1.0 else b * s for b, s in zip(blocks, col_scales)]
            chunk = jnp.concatenate(blocks, axis=1)
        dst_ref[c * chunk_rows:(c + 1) * chunk_rows, 0:chunk.shape[1]] = chunk.astype(jnp.bfloat16)

    _stream_rows(src_hbm, stage, sem, chunk_rows, cast)


def _load_folded_pool(w_pool_out_hbm, pool_w_ref, pool_scale_ref, dst_ref, stage, sem, chunk_rows):
    groups_per_chunk = chunk_rows // POOL_GROUP_DIM

    def fold(c, chunk):
        for j in range(groups_per_chunk):
            g = c * groups_per_chunk + j
            scaled = pool_w_ref[g] * pool_scale_ref[g]
            rows = chunk[j * POOL_GROUP_DIM:(j + 1) * POOL_GROUP_DIM, :]
            folded = jnp.dot(scaled, rows, precision=lax.Precision.HIGHEST,
                             preferred_element_type=jnp.float32)
            dst_ref[g * POOL_GROUP_DIM:(g + 1) * POOL_GROUP_DIM, :] = (
                folded * MERGE_INPUT_SCALE).astype(jnp.bfloat16)

    _stream_rows(w_pool_out_hbm, stage, sem, chunk_rows, fold)


def _mixer_kernel(x_ref, g_pre_ref, w_in_hbm, pool_w_ref, pool_scale_ref, w_pool_out_hbm,
                  conv_w_ref, conv_b_ref, ln_g_ref, ln_b_ref, w_conv_out_hbm, w_o_hbm,
                  g_post_ref, o_ref,
                  w_in_ref, w_pool_ref, w_conv_out_ref, w_o_ref, stage_in, stage_out, sem,
                  h_ref, pool_ext, pooled_ref, conv_x, conv_out, y_ref, gate_ref, merged_ref,
                  *, tile):
    t = pl.program_id(1)
    conv_rows = CONV_HALO + tile

    @pl.when((pl.program_id(0) == 0) & (t == 0))
    def _():
        _load_as_bf16(w_in_hbm, w_in_ref, stage_in, sem, W_IN_CHUNK_ROWS,
                      col_starts=W_IN_COL_STARTS, col_scales=W_IN_COL_SCALES)
        _load_folded_pool(w_pool_out_hbm, pool_w_ref, pool_scale_ref, w_pool_ref, stage_out, sem,
                          W_OUT_CHUNK_ROWS)
        _load_as_bf16(w_conv_out_hbm, w_conv_out_ref, stage_out, sem, W_OUT_CHUNK_ROWS,
                      scale=MERGE_INPUT_SCALE)
        _load_as_bf16(w_o_hbm, w_o_ref, stage_out, sem, W_OUT_CHUNK_ROWS)

    @pl.when(t == 0)
    def _():
        pool_ext[0:POOL_HALO, :] = jnp.zeros((POOL_HALO, POOL_WIDTH), jnp.float32)
        conv_x[0, 0:CONV_HALO, 0:CONV_WIDTH] = jnp.zeros((CONV_HALO, CONV_WIDTH), jnp.float32)

    def project(c0):
        return _dot(h_ref[...], w_in_ref[:, c0:c0 + PROJ_COLS])

    h_ref[...] = _rmsnorm(x_ref[...], g_pre_ref[...]).astype(jnp.bfloat16)

    half = CONV_WIDTH // 2
    for j in range(2):
        glu = project(GLU_OFF + j * PROJ_COLS)
        conv_x[0, CONV_HALO:CONV_HALO + tile, j * half:(j + 1) * half] = (
            glu[:, 0:half] * (jnp.tanh(glu[:, half:PROJ_COLS]) + 1.0))
        if j == 0:
            pool_ext[POOL_HALO:POOL_HALO + tile, :] = project(0)
        for c0 in range(j * half, (j + 1) * half, LANES):
            cols = slice(c0, c0 + LANES)
            x0 = conv_x[0, :, cols]
            for s in range(1, SUBLANES):
                conv_x[s, :, cols] = pltpu.roll(x0, conv_rows - s, axis=0)

    head_frame = t * tile + lax.broadcasted_iota(jnp.int32, (POOL_HALO, 1), 0)
    head_rows = slice(POOL_HALO, 2 * POOL_HALO)
    for g, window in enumerate(POOL_WINDOWS):
        cols = slice(g * POOL_GROUP_DIM, (g + 1) * POOL_GROUP_DIM)
        ext = pool_ext[:, cols]
        s = ext
        step = 1
        while step < window:
            s = s + pltpu.roll(s, step, axis=0)
            step *= 2
        inv_count = 1.0 / jnp.minimum(head_frame + 1, window).astype(jnp.float32)
        head = s[head_rows, :] * inv_count - ext[head_rows, :]
        body = s[2 * POOL_HALO:, :] * (1.0 / window) - ext[2 * POOL_HALO:, :]
        pooled_ref[0:POOL_HALO, cols] = head.astype(jnp.bfloat16)
        pooled_ref[POOL_HALO:, cols] = body.astype(jnp.bfloat16)

    first_tap = CONV_HALO - (CONV_K - 1)
    for r0 in range(0, tile, CONV_ROWS):
        accs = []
        for c0 in range(0, CONV_WIDTH, LANES):
            cols = slice(c0, c0 + LANES)
            acc = jnp.broadcast_to(conv_b_ref[:, cols], (CONV_ROWS, LANES))
            for k in range(CONV_K):
                q, s = divmod(first_tap + k, SUBLANES)
                lo = r0 + q * SUBLANES
                acc = acc + conv_w_ref[k:k + 1, cols] * conv_x[s, lo:lo + CONV_ROWS, cols]
            accs.append(acc)
        conv_out[r0:r0 + CONV_ROWS, :] = jnp.concatenate(accs, axis=-1)

    def gate_piece(i, after):
        lhs = h_ref[...]
        if after is not None:
            lhs = lhs + _repeat(_zero_after(after).astype(jnp.bfloat16), tile, D_MODEL)
        c0 = i * PROJ_COLS
        piece = jnp.tanh(_dot(lhs, w_in_ref[:, GATE_A_OFF + c0:GATE_A_OFF + c0 + PROJ_COLS]))
        gate_ref[:, c0:c0 + PROJ_COLS] = piece
        return piece[tile - PACKED_ROWS:, PROJ_COLS - LANES:]

    half_ln_g = 0.5 * ln_g_ref[...]
    half_ln_b = 0.5 * ln_b_ref[...]

    def norm_stage(i, after):
        for r0 in range(i * stage_rows, (i + 1) * stage_rows, CONV_ROWS):
            c = conv_out[r0:r0 + CONV_ROWS, :]
            if after is not None:
                c = c + _repeat(_zero_after(after), CONV_ROWS, CONV_WIDTH)
            mu = jnp.mean(c, axis=-1, keepdims=True)
            cc = c - mu
            var = jnp.mean(cc * cc, axis=-1, keepdims=True)
            h = cc * lax.rsqrt(var + EPS) * half_ln_g + half_ln_b
            y = h * jnp.tanh(h) + h
            y_ref[r0:r0 + CONV_ROWS, :] = y.astype(jnp.bfloat16)
        return y[CONV_ROWS - PACKED_ROWS:, CONV_WIDTH - LANES:]

    n_pieces = 2 * D_MODEL // PROJ_COLS
    stage_rows = tile // n_pieces
    mxu_done = vpu_done = None
    for i in range(n_pieces):
        piece_done = gate_piece(i, vpu_done)
        vpu_done = norm_stage(i, mxu_done)
        mxu_done = piece_done

    for c0 in range(0, D_MODEL, PROJ_COLS):
        cols = slice(c0, c0 + PROJ_COLS)
        a = _dot(pooled_ref[...], w_pool_ref[:, cols])
        b = _dot(y_ref[...], w_conv_out_ref[:, cols])
        merged = ((gate_ref[:, cols] + 1.0) * a
                  + (gate_ref[:, D_MODEL + c0:D_MODEL + c0 + PROJ_COLS] + 1.0) * b)
        merged_ref[:, cols] = merged.astype(jnp.bfloat16)
    mix = _dot(merged_ref[...], w_o_ref[...])
    o_ref[...] = x_ref[...] + _rmsnorm(mix, g_post_ref[...])

    pool_ext[0:POOL_HALO, :] = pool_ext[tile:tile + POOL_HALO, :]
    conv_x[0, 0:CONV_HALO, 0:CONV_WIDTH] = conv_x[0, tile:tile + CONV_HALO, 0:CONV_WIDTH]


def _mlp_kernel(x_ref, g_pre_ref, w_up_hbm, w_down_hbm, g_post_ref, o_ref,
                w_up_ref, w_down_ref, stage_up, stage_down, sem):
    @pl.when(pl.program_id(0) == 0)
    def _():
        _load_as_bf16(w_up_hbm, w_up_ref, stage_up, sem, W_UP_CHUNK_ROWS)
        _load_as_bf16(w_down_hbm, w_down_ref, stage_down, sem, W_DOWN_CHUNK_ROWS)

    x = x_ref[...]
    h = _rmsnorm(x, g_pre_ref[...]).astype(jnp.bfloat16)
    acc = None
    for c0 in range(0, D_FF, FF_CHUNK):
        a = jnp.maximum(_dot(h, w_up_ref[:, c0:c0 + FF_CHUNK]), 0.0)
        p = _dot((a * a).astype(jnp.bfloat16), w_down_ref[c0:c0 + FF_CHUNK, 0:D_MODEL])
        acc = p if acc is None else acc + p
    o_ref[...] = x + _rmsnorm(acc, g_post_ref[...])


def _resident(shape):
    return pl.BlockSpec(shape, lambda *_: (0,) * len(shape), pipeline_mode=pl.Buffered(1))


def _mixer(x, g_pre, w_in, pool_w, pool_scale, w_pool_out, conv_w, layer, conv_b, ln_g, ln_b,
           w_conv_out, w_o, g_post):
    batch, seq, d = x.shape
    tile = MIXER_TILE
    assert seq % tile == 0 and d == D_MODEL
    x_spec = pl.BlockSpec((None, tile, d), lambda b, t: (b, t, 0))
    operands = (g_pre, w_in, pool_w, pool_scale, w_pool_out, conv_w, conv_b, ln_g, ln_b,
                w_conv_out, w_o, g_post)
    in_hbm = (w_in, w_pool_out, w_conv_out, w_o)
    hbm = pl.BlockSpec(memory_space=pl.ANY)
    conv_w_spec = pl.BlockSpec((None,) + conv_w.shape[1:], lambda *_: (layer, 0, 0),
                               pipeline_mode=pl.Buffered(1))

    def spec(a):
        if a is conv_w:
            return conv_w_spec
        return hbm if any(a is w for w in in_hbm) else _resident(a.shape)

    return pl.pallas_call(
        functools.partial(_mixer_kernel, tile=tile),
        grid=(batch, seq // tile),
        in_specs=[x_spec] + [spec(a) for a in operands],
        out_specs=x_spec,
        out_shape=jax.ShapeDtypeStruct(x.shape, x.dtype),
        scratch_shapes=[
            pltpu.VMEM(w_in.shape, jnp.bfloat16),
            pltpu.VMEM(w_pool_out.shape, jnp.bfloat16),
            pltpu.VMEM(w_conv_out.shape, jnp.bfloat16),
            pltpu.VMEM(w_o.shape, jnp.bfloat16),
            pltpu.VMEM((2, W_IN_CHUNK_ROWS, w_in.shape[1]), jnp.float32),
            pltpu.VMEM((2, W_OUT_CHUNK_ROWS, d), jnp.float32),
            pltpu.SemaphoreType.DMA((2,)),
            pltpu.VMEM((tile, d), jnp.bfloat16),
            pltpu.VMEM((POOL_HALO + tile, POOL_WIDTH), jnp.float32),
            pltpu.VMEM((tile, POOL_WIDTH), jnp.bfloat16),
            pltpu.VMEM((SUBLANES, CONV_HALO + tile, CONV_PITCH), jnp.float32),
            pltpu.VMEM((tile, CONV_WIDTH), jnp.float32),
            pltpu.VMEM((tile, CONV_WIDTH), jnp.bfloat16),
            pltpu.VMEM((tile, 2 * d), jnp.float32),
            pltpu.VMEM((tile, d), jnp.bfloat16),
        ],
        compiler_params=pltpu.CompilerParams(
            dimension_semantics=("arbitrary", "arbitrary"),
            vmem_limit_bytes=VMEM_LIMIT_BYTES),
        name="mixer",
    )(x, *operands)


def _mlp(x, g_pre, w_up, w_down, g_post):
    tokens, d = x.shape
    tile = MLP_TILE
    assert tokens % tile == 0 and d == D_MODEL
    x_spec = pl.BlockSpec((tile, d), lambda i: (i, 0))
    operands = (g_pre, w_up, w_down, g_post)
    return pl.pallas_call(
        _mlp_kernel,
        grid=(tokens // tile,),
        in_specs=[x_spec, _resident(g_pre.shape), pl.BlockSpec(memory_space=pl.ANY),
                  pl.BlockSpec(memory_space=pl.ANY), _resident(g_post.shape)],
        out_specs=x_spec,
        out_shape=jax.ShapeDtypeStruct(x.shape, x.dtype),
        scratch_shapes=[
            pltpu.VMEM((d, D_FF + LANES), jnp.bfloat16),
            pltpu.VMEM((D_FF, d + LANES), jnp.bfloat16),
            pltpu.VMEM((2, W_UP_CHUNK_ROWS, w_up.shape[1]), jnp.float32),
            pltpu.VMEM((2, W_DOWN_CHUNK_ROWS, w_down.shape[1]), jnp.float32),
            pltpu.SemaphoreType.DMA((2,)),
        ],
        compiler_params=pltpu.CompilerParams(
            dimension_semantics=("arbitrary",),
            vmem_limit_bytes=VMEM_LIMIT_BYTES),
        name="mlp",
    )(x, *operands)


def kernel(x, norm_mix_pre, w_in, pool_w, pool_scale, w_pool_out, conv_w, conv_b, conv_ln_g,
           conv_ln_b, w_conv_out, w_o, norm_mix_post, norm_mlp_pre, w_up, w_down, norm_mlp_post):
    batch, seq, d = x.shape
    row = lambda p: p.reshape(1, -1)
    for l in range(w_in.shape[0]):
        x = _mixer(x, row(norm_mix_pre[l]), w_in[l], pool_w[l],
                   pool_scale[l].reshape(len(POOL_WINDOWS), 1, POOL_GROUP_DIM), w_pool_out[l],
                   conv_w, l, row(conv_b[l]), row(conv_ln_g[l]), row(conv_ln_b[l]), w_conv_out[l],
                   w_o[l], row(norm_mix_post[l]))
        x = _mlp(x.reshape(batch * seq, d), row(norm_mlp_pre[l]), w_up[l], w_down[l],
                 row(norm_mlp_post[l])).reshape(batch, seq, d)
    return x
```
